```python
import jax, jax.numpy as jnp
from jax import lax
import numpy as np

D_MODEL = 1024
BATCH = 1
SEQ = 16384
DEPTH = 2
DEC_BATCH = 32
DEC_SEQ = 8
PAST_LEN = 16384
PAGE_SIZE = 128

LRU_WIDTH = D_MODEL // 2
LRU_BLOCKS = 8
LRU_BD = LRU_WIDTH // LRU_BLOCKS
CONV_W = 4
LRU_C = 8.0
HG_HEADS = 4
HG_DK = 128
HG_DV = 128
HG_WIDTH = HG_HEADS * HG_DK
HG_CHUNK = 64
DIL_GROUPS = ((128, 1), (512, 4), (2048, 16))
N_GROUPS = 3
DIL_HPG = 4
DIL_DH = 64
DIL_HEADS = N_GROUPS * DIL_HPG
DIL_WIDTH = DIL_HEADS * DIL_DH
DIL_OUT = DIL_HPG * DIL_DH
DIL_QB = 128
N_BRANCH = 3
D_FF = 4 * D_MODEL
EPS = 1e-6
NEG = -1e30
SPLIT_SIZES = (LRU_WIDTH,) + (HG_WIDTH,) * 4 + (DIL_WIDTH,) * 3 + (N_BRANCH * D_MODEL,)
N_IN = sum(SPLIT_SIZES)
F32 = jnp.float32

kernel_name = 'hybrid_rglru_hgrn2_dilated_decode_step'


def _rmsnorm(x, g):
    xf = x.astype(F32)
    y = xf * lax.rsqrt(jnp.mean(xf * xf, axis=-1, keepdims=True) + EPS)
    return (y * g.astype(F32)).astype(x.dtype)


def _masked_softmax(s, valid):
    s = jnp.where(valid, s, NEG)
    m = jnp.max(s, axis=-1, keepdims=True)
    p = jnp.where(valid, jnp.exp(s - m), 0.0)
    l = jnp.sum(p, axis=-1, keepdims=True)
    return p / l, (m + jnp.log(l))[..., 0]


def _lin_comb(left, right):
    a1, b1 = left
    a2, b2 = right
    return a1 * a2, a2 * b1 + b2


def _rglru(u, conv_st, h0, pos0, conv_w, conv_b, wr, br, wi, bi, lam):
    bsz, t_len, width = u.shape
    ext = jnp.concatenate([conv_st.astype(u.dtype), u], axis=1)
    xc = conv_b + sum(ext[:, j:j + t_len] * conv_w[j] for j in range(CONV_W))
    conv_new = ext[:, t_len:]
    xf = xc.astype(F32)
    xb = xf.reshape(bsz, t_len, LRU_BLOCKS, LRU_BD)
    r = jax.nn.sigmoid(jnp.einsum('btnd,nde->btne', xb, wr.astype(F32)).reshape(bsz, t_len, width) + br.astype(F32))
    i = jax.nn.sigmoid(jnp.einsum('btnd,nde->btne', xb, wi.astype(F32)).reshape(bsz, t_len, width) + bi.astype(F32))
    log_a = -LRU_C * r * jax.nn.softplus(-lam.astype(F32))
    pos = pos0 + jnp.arange(t_len)
    mult = jnp.where((pos == 0)[None, :, None], 1.0, jnp.sqrt(-jnp.expm1(2.0 * log_a)))
    a_cum, b_cum = lax.associative_scan(_lin_comb, (jnp.exp(log_a), mult * i * xf), axis=1)
    hs = b_cum + a_cum * h0.astype(F32)[:, None]
    return hs.astype(u.dtype), conv_new, hs[:, -1].astype(h0.dtype)


def _chunk_gated_scan(q, k, v, log_f, s0):
    bsz, t_len, nh, dk = q.shape
    dv = v.shape[-1]
    c = min(HG_CHUNK, t_len)
    n = -(-t_len // c)
    pad = n * c - t_len

    def to_chunks(z):
        z = jnp.pad(z, ((0, 0), (0, pad), (0, 0), (0, 0)))
        return z.reshape(bsz, n, c, nh, z.shape[-1]).swapaxes(0, 1)

    causal = jnp.tril(jnp.ones((c, c), dtype=bool))[None, :, :, None, None]

    def step(state, blk):
        qc, kc, vc, gc = blk
        b = jnp.cumsum(gc, axis=1)
        diff = b[:, :, None] - b[:, None, :]
        decay = jnp.where(causal, jnp.exp(jnp.where(causal, diff, 0.0)), 0.0)
        attn = jnp.einsum('bthd,btshd,bshd->bhts', qc, decay, kc)
        o = jnp.einsum('bhts,bshv->bthv', attn, vc) + jnp.einsum('bthd,bhdv->bthv', qc * jnp.exp(b), state)
        b_last = b[:, -1]
        state = jnp.exp(b_last)[..., None] * state + jnp.einsum('bshd,bshv->bhdv', kc * jnp.exp(b_last[:, None] - b), vc)
        return state, o

    s_fin, o = lax.scan(step, s0, (to_chunks(q), to_chunks(k), to_chunks(v), to_chunks(log_f)))
    o = o.swapaxes(0, 1).reshape(bsz, n * c, nh, dv)[:, :t_len]
    return o, s_fin


def _hgrn2(q_raw, f_raw, i_raw, g_raw, s0, lb, norm_g):
    bsz, t_len, _ = q_raw.shape
    shp = (bsz, t_len, HG_HEADS, HG_DK)
    q = jax.nn.silu(q_raw.astype(F32)).reshape(shp)
    zf = f_raw.astype(F32)
    lbf = lb.astype(F32)
    log_f = jnp.log(lbf + (1.0 - lbf) * jax.nn.sigmoid(zf)).reshape(shp)
    k = ((1.0 - lbf) * jax.nn.sigmoid(-zf)).reshape(shp)
    v = i_raw.astype(F32).reshape(bsz, t_len, HG_HEADS, HG_DV)
    o, s_fin = _chunk_gated_scan(q, k, v, log_f, s0.astype(F32))
    o = o * lax.rsqrt(jnp.mean(o * o, axis=-1, keepdims=True) + EPS) * norm_g.astype(F32)
    o = o.reshape(bsz, t_len, HG_WIDTH) * jax.nn.silu(g_raw.astype(F32))
    return o.astype(q_raw.dtype), s_fin.astype(s0.dtype)


def _dilated_prompt(q, k, v, window, dil):
    bsz, s_len, nh, dh = q.shape
    reach = window // dil
    span = dil * DIL_QB
    s_pad = -(-s_len // span) * span
    m_len = s_pad // dil
    nb = m_len // DIL_QB

    def to_sub(z):
        z = jnp.pad(z.astype(F32), ((0, 0), (0, s_pad - s_len), (0, 0), (0, 0)))
        z = z.reshape(bsz, m_len, dil, nh, dh).transpose(0, 2, 1, 3, 4)
        return z.reshape(bsz, dil, nb, DIL_QB, nh, dh)

    def with_prev(z):
        prev = jnp.concatenate([jnp.zeros_like(z[:, :, :1]), z[:, :, :-1]], axis=2)
        return jnp.concatenate([prev, z], axis=3)

    qs = to_sub(q)
    kk = with_prev(to_sub(k))
    vv = with_prev(to_sub(v))
    s = jnp.einsum('brnqhd,brnkhd->brnhqk', qs, kk)
    qi = jnp.arange(DIL_QB)[:, None]
    kj = jnp.arange(2 * DIL_QB)[None, :] - DIL_QB
    dist = qi - kj
    blk = jnp.arange(nb)[:, None, None] * DIL_QB
    valid = (dist >= 0) & (dist <= reach) & (blk + kj >= 0)
    p, lse = _masked_softmax(s, valid[None, None, :, None])
    o = jnp.einsum('brnhqk,brnkhd->brnqhd', p, vv)
    o = o.reshape(bsz, dil, m_len, nh, dh).transpose(0, 2, 1, 3, 4).reshape(bsz, s_pad, nh, dh)
    lse = lse.transpose(0, 1, 2, 4, 3).reshape(bsz, dil, m_len, nh).transpose(0, 2, 1, 3).reshape(bsz, s_pad, nh)
    return o[:, :s_len], lse[:, :s_len]


def _dilated_decode(q, k, v, buf, window, dil):
    t_len = q.shape[1]
    w_len = buf.shape[1]
    kc = jnp.concatenate([buf[:, :, 0], k.astype(buf.dtype)], axis=1)
    vc = jnp.concatenate([buf[:, :, 1], v.astype(buf.dtype)], axis=1)
    n_keys = window // dil + 1
    idx = w_len + jnp.arange(t_len)[:, None] - dil * jnp.arange(n_keys)[None, :]
    valid = idx >= 0
    idx = jnp.maximum(idx, 0)
    kg = kc[:, idx].astype(F32)
    vg = vc[:, idx].astype(F32)
    s = jnp.einsum('bthd,btkhd->bthk', q, kg)
    p, lse = _masked_softmax(s, valid[None, :, None, :])
    o = jnp.einsum('bthk,btkhd->bthd', p, vg)
    new_buf = jnp.stack([kc, vc], axis=2)[:, t_len:]
    return o, lse, new_buf


def _dilated(q_c, k_c, v_c, bufs):
    bsz, t_len, _ = q_c.shape
    shp = (bsz, t_len, DIL_HEADS, DIL_DH)
    q = q_c.astype(F32).reshape(shp) * (DIL_DH ** -0.5)
    k = k_c.reshape(shp)
    v = v_c.reshape(shp)
    outs, lses, new_bufs = [], [], []
    for g, (window, dil) in enumerate(DIL_GROUPS):
        sl = slice(g * DIL_HPG, (g + 1) * DIL_HPG)
        if bufs is None:
            o, lse = _dilated_prompt(q[:, :, sl], k[:, :, sl], v[:, :, sl], window, dil)
            kv = jnp.stack([k[:, :, sl], v[:, :, sl]], axis=2)
            new_bufs.append(kv[:, t_len - min(window, t_len):])
        else:
            o, lse, nbuf = _dilated_decode(q[:, :, sl], k[:, :, sl], v[:, :, sl], bufs[g], window, dil)
            new_bufs.append(nbuf)
        outs.append(o)
        lses.append(lse)
    w = jax.nn.softmax(jnp.stack(lses, axis=0), axis=0)
    o = jnp.sum(w[..., None] * jnp.stack(outs, axis=0), axis=0)
    return o.reshape(bsz, t_len, DIL_OUT).astype(q_c.dtype), tuple(new_bufs)


def _layer(x, pos0, conv_st, lru_st, hg_st, bufs, lb, lw):
    (n_mix_pre, n_mix_post, n_mlp_pre, n_mlp_post, w_in, b_gate, conv_w, conv_b, lru_wr, lru_br,
     lru_wi, lru_bi, lru_lambda, hg_norm, w_br_a, w_br_b, w_br_c, w_out, w_up, w_down) = lw
    bsz, t_len, _ = x.shape
    h = _rmsnorm(x, n_mix_pre)
    z = h @ w_in
    cuts, acc = [], 0
    for size in SPLIT_SIZES[:-1]:
        acc += size
        cuts.append(acc)
    u_a, q_b, f_b, i_b, g_b, q_c, k_c, v_c, gate_pre = jnp.split(z, cuts, axis=-1)
    y_a, conv_new, lru_new = _rglru(u_a, conv_st, lru_st, pos0, conv_w, conv_b, lru_wr, lru_br, lru_wi, lru_bi, lru_lambda)
    y_b, hg_new = _hgrn2(q_b, f_b, i_b, g_b, hg_st, lb, hg_norm)
    y_c, bufs_new = _dilated(q_c, k_c, v_c, bufs)
    gates = jax.nn.sigmoid((gate_pre + b_gate).astype(F32)).astype(x.dtype).reshape(bsz, t_len, N_BRANCH, D_MODEL)
    mixed = gates[:, :, 0] * (y_a @ w_br_a) + gates[:, :, 1] * (y_b @ w_br_b) + gates[:, :, 2] * (y_c @ w_br_c)
    x = x + _rmsnorm(mixed @ w_out, n_mix_post)
    hm = _rmsnorm(x, n_mlp_pre)
    x = x + _rmsnorm(jnp.square(jax.nn.relu(hm @ w_up)) @ w_down, n_mlp_post)
    return x, conv_new, lru_new, hg_new, bufs_new


def setup_inputs(seed: int = 0) -> dict:
    key = jax.random.key(seed)
    keys = jax.random.split(key, 40)
    counter = [0]

    def nxt():
        counter[0] += 1
        return keys[counter[0] - 1]

    def nrm(shape, scale):
        return jax.random.normal(nxt(), shape, jnp.float32) * scale

    def gain(shape):
        return 1.0 + nrm(shape, 0.02)

    wc = [min(w, PAST_LEN) for w, _ in DIL_GROUPS]
    x_prompt = nrm((BATCH, SEQ, D_MODEL), 1.0)
    x_sample = nrm((DEC_BATCH, DEC_SEQ, D_MODEL), 1.0)
    state_conv = nrm((DEPTH, DEC_BATCH, CONV_W - 1, LRU_WIDTH), 1.0)
    state_lru = nrm((DEPTH, DEC_BATCH, LRU_WIDTH), 0.5)
    state_hgrn = nrm((DEPTH, DEC_BATCH, HG_HEADS, HG_DK, HG_DV), 0.3)
    cache_win128 = nrm((DEPTH, DEC_BATCH, wc[0], 2, DIL_HPG, DIL_DH), 1.0)
    cache_win512 = nrm((DEPTH, DEC_BATCH, wc[1], 2, DIL_HPG, DIL_DH), 1.0)
    cache_win2048 = nrm((DEPTH, DEC_BATCH, wc[2], 2, DIL_HPG, DIL_DH), 1.0)
    norm_mix_pre = gain((DEPTH, D_MODEL))
    norm_mix_post = gain((DEPTH, D_MODEL))
    norm_mlp_pre = gain((DEPTH, D_MODEL))
    norm_mlp_post = gain((DEPTH, D_MODEL))
    w_in = nrm((DEPTH, D_MODEL, N_IN), D_MODEL ** -0.5)
    b_gate = nrm((DEPTH, N_BRANCH * D_MODEL), 0.01)
    conv_w = nrm((DEPTH, CONV_W, LRU_WIDTH), CONV_W ** -0.5)
    conv_b = nrm((DEPTH, LRU_WIDTH), 0.01)
    lru_wr = nrm((DEPTH, LRU_BLOCKS, LRU_BD, LRU_BD), LRU_BD ** -0.5)
    lru_br = nrm((DEPTH, LRU_WIDTH), 0.01)
    lru_wi = nrm((DEPTH, LRU_BLOCKS, LRU_BD, LRU_BD), LRU_BD ** -0.5)
    lru_bi = nrm((DEPTH, LRU_WIDTH), 0.01)
    a0 = jax.random.uniform(nxt(), (DEPTH, LRU_WIDTH), jnp.float32, 0.9, 0.999)
    s_a = a0 ** (1.0 / LRU_C)
    lru_lambda = jnp.log(s_a) - jnp.log1p(-s_a)
    hgrn_lb_raw = nrm((DEPTH, HG_WIDTH), 0.5)
    hgrn_norm = gain((DEPTH, HG_DV))
    w_br_a = nrm((DEPTH, LRU_WIDTH, D_MODEL), LRU_WIDTH ** -0.5)
    w_br_b = nrm((DEPTH, HG_WIDTH, D_MODEL), HG_WIDTH ** -0.5)
    w_br_c = nrm((DEPTH, DIL_OUT, D_MODEL), DIL_OUT ** -0.5)
    w_out = nrm((DEPTH, D_MODEL, D_MODEL), D_MODEL ** -0.5)
    w_mlp_up = nrm((DEPTH, D_MODEL, D_FF), D_MODEL ** -0.5)
    w_mlp_down = nrm((DEPTH, D_FF, D_MODEL), D_FF ** -0.5)
    return {'x_prompt': x_prompt, 'x_sample': x_sample, 'state_conv': state_conv, 'state_lru': state_lru,
            'state_hgrn': state_hgrn, 'cache_win128': cache_win128, 'cache_win512': cache_win512,
            'cache_win2048': cache_win2048, 'norm_mix_pre': norm_mix_pre, 'norm_mix_post': norm_mix_post,
            'norm_mlp_pre': norm_mlp_pre, 'norm_mlp_post': norm_mlp_post, 'w_in': w_in, 'b_gate': b_gate,
            'conv_w': conv_w, 'conv_b': conv_b, 'lru_wr': lru_wr, 'lru_br': lru_br, 'lru_wi': lru_wi,
            'lru_bi': lru_bi, 'lru_lambda': lru_lambda, 'hgrn_lb_raw': hgrn_lb_raw, 'hgrn_norm': hgrn_norm,
            'w_br_a': w_br_a, 'w_br_b': w_br_b, 'w_br_c': w_br_c, 'w_out': w_out,
            'w_mlp_up': w_mlp_up, 'w_mlp_down': w_mlp_down}


def reference(x_prompt, x_sample, state_conv, state_lru, state_hgrn, cache_win128, cache_win512, cache_win2048,
              norm_mix_pre, norm_mix_post, norm_mlp_pre, norm_mlp_post, w_in, b_gate, conv_w, conv_b,
              lru_wr, lru_br, lru_wi, lru_bi, lru_lambda, hgrn_lb_raw, hgrn_norm, w_br_a, w_br_b, w_br_c,
              w_out, w_mlp_up, w_mlp_down):
    lb_soft = jax.nn.softmax(hgrn_lb_raw.astype(F32), axis=0)
    lb_all = jnp.cumsum(lb_soft, axis=0) - lb_soft[0]
    bp = x_prompt.shape[0]
    dt = x_prompt.dtype
    yp, ys = x_prompt, x_sample
    new_p = [[] for _ in range(6)]
    new_s = [[] for _ in range(6)]
    for l in range(DEPTH):
        lw = (norm_mix_pre[l], norm_mix_post[l], norm_mlp_pre[l], norm_mlp_post[l], w_in[l], b_gate[l],
              conv_w[l], conv_b[l], lru_wr[l], lru_br[l], lru_wi[l], lru_bi[l], lru_lambda[l], hgrn_norm[l],
              w_br_a[l], w_br_b[l], w_br_c[l], w_out[l], w_mlp_up[l], w_mlp_down[l])
        yp, c_p, h_p, s_p, b_p = _layer(
            yp, 0,
            jnp.zeros((bp, CONV_W - 1, LRU_WIDTH), dt),
            jnp.zeros((bp, LRU_WIDTH), dt),
            jnp.zeros((bp, HG_HEADS, HG_DK, HG_DV), dt),
            None, lb_all[l], lw)
        ys, c_s, h_s, s_s, b_s = _layer(
            ys, PAST_LEN, state_conv[l], state_lru[l], state_hgrn[l],
            (cache_win128[l], cache_win512[l], cache_win2048[l]), lb_all[l], lw)
        for lst, val in zip(new_p, (c_p, h_p, s_p) + b_p):
            lst.append(val)
        for lst, val in zip(new_s, (c_s, h_s, s_s) + b_s):
            lst.append(val)
    conv_p, lru_p, hgrn_p, win128_p, win512_p, win2048_p = [jnp.stack(v, axis=0) for v in new_p]
    conv_s, lru_s, hgrn_s, win128_s, win512_s, win2048_s = [jnp.stack(v, axis=0) for v in new_s]
    return (yp, ys, conv_p, conv_s, lru_p, lru_s, hgrn_p, hgrn_s,
            win128_p, win128_s, win512_p, win512_s, win2048_p, win2048_s)
```

```python
import functools

import jax
import jax.numpy as jnp
from jax import lax
from jax.experimental import pallas as pl
from jax.experimental.pallas import tpu as pltpu

F32 = jnp.float32
BF16 = jnp.bfloat16

D_MODEL = 1024
LRU_WIDTH = 512
LRU_C = 8.0
CONV_W = 4
HG_HEADS = 4
HG_D = 128
DIL_GROUPS = ((128, 1), (512, 4), (2048, 16))
DIL_HPG = 4
DIL_DH = 64
DIL_GW = DIL_HPG * DIL_DH
DIL_QB = 128
N_BRANCH = 3
D_FF = 4096
EPS = 1e-6
NEG = -1e30

GATE_W = N_BRANCH * D_MODEL
UA_OFF = GATE_W
QB_OFF = UA_OFF + LRU_WIDTH
FB_OFF = QB_OFF + 512
IB_OFF = FB_OFF + 512
GB_OFF = IB_OFF + 512
QC_OFF = GB_OFF + 512
KC_OFF = QC_OFF + 768
VC_OFF = KC_OFF + 768
N_IN = VC_OFF + 768
N_PAD = 8192

V7X_LANES = 128
V7X_SUBLANES = 8
HG_CHUNK = 128
HG_SUB = 16
VMEM_LIMIT = 56 * 1024 * 1024


def _cparams(sem):
    return pltpu.CompilerParams(dimension_semantics=sem, vmem_limit_bytes=VMEM_LIMIT)


def _sigmoid(x):
    return 1.0 / (1.0 + jnp.exp(-x))


def _dot(a, b):
    return jnp.dot(a, b, preferred_element_type=F32)


def _dot_nt(a, b):
    return lax.dot_general(a, b, (((1,), (1,)), ((), ())), preferred_element_type=F32)


def _dot_tn(a, b):
    return lax.dot_general(a, b, (((0,), (0,)), ((), ())), preferred_element_type=F32)


def _rms(x, g):
    ms = jnp.mean(x * x, axis=-1, keepdims=True)
    return x * lax.rsqrt(ms + EPS) * g


def _inproj_kernel(x_ref, g_ref, w_ref, z_ref, h_scr):
    @pl.when(pl.program_id(1) == 0)
    def _():
        h_scr[...] = _rms(x_ref[...], g_ref[...]).astype(BF16)

    z_ref[...] = _dot(h_scr[...], w_ref[...])


def _inproj(x, g, w, tm, tn):
    m = x.shape[0]
    return pl.pallas_call(
        _inproj_kernel,
        grid=(m // tm, N_PAD // tn),
        in_specs=[
            pl.BlockSpec((tm, D_MODEL), lambda i, j: (i, 0)),
            pl.BlockSpec((1, D_MODEL), lambda i, j: (0, 0)),
            pl.BlockSpec((D_MODEL, tn), lambda i, j: (0, j)),
        ],
        out_specs=pl.BlockSpec((tm, tn), lambda i, j: (i, j)),
        out_shape=jax.ShapeDtypeStruct((m, N_PAD), F32),
        scratch_shapes=[pltpu.VMEM((tm, D_MODEL), BF16)],
        compiler_params=_cparams(("parallel", "arbitrary")),
        name="inproj",
    )(x, g, w)


def _lru_kernel(u_ref, cst_ref, h0_ref, cw_ref, cb_ref, wr_ref, wi_ref, br_ref, bi_ref, lam_ref,
                y_ref, ext_scr, a_scr, b_scr, h_scr, *, tb, first_pos_is_zero):
    j = pl.program_id(1)

    @pl.when(j == 0)
    def _():
        ext_scr[0:8, :] = jnp.zeros((8, LRU_WIDTH), F32)
        ext_scr[5:8, :] = cst_ref[0]
        h_scr[...] = h0_ref[0]

    u = u_ref[...]
    ext_scr[8:8 + tb, :] = u
    cw = cw_ref[...]
    xc = (cb_ref[...] + ext_scr[5:5 + tb, :] * cw[0:1] + ext_scr[6:6 + tb, :] * cw[1:2]
          + ext_scr[7:7 + tb, :] * cw[2:3] + u * cw[3:4])
    ext_scr[5:8, :] = u[tb - 3:tb]

    xb = xc.astype(BF16)
    r = _sigmoid(_dot(xb, wr_ref[...]) + br_ref[...])
    i = _sigmoid(_dot(xb, wi_ref[...]) + bi_ref[...])
    nl = -lam_ref[...]
    softplus = jnp.maximum(nl, 0.0) + jnp.log(1.0 + jnp.exp(-jnp.abs(nl)))
    log_a = -LRU_C * r * softplus
    a = jnp.exp(log_a)
    mult = jnp.sqrt(1.0 - jnp.exp(2.0 * log_a))
    if first_pos_is_zero:
        row = lax.broadcasted_iota(jnp.int32, (tb, LRU_WIDTH), 0)
        mult = jnp.where((row == 0) & (j == 0), 1.0, mult)
    a_scr[...] = a
    b_scr[...] = mult * i * xc

    sub = lax.broadcasted_iota(jnp.int32, (8, LRU_WIDTH), 0)

    def body(g, h):
        r0 = pl.multiple_of(g * 8, 8)
        a8 = a_scr[pl.ds(r0, 8), :]
        b8 = b_scr[pl.ds(r0, 8), :]
        y8 = jnp.zeros((8, LRU_WIDTH), F32)
        for k in range(8):
            h = a8[k:k + 1] * h + b8[k:k + 1]
            y8 = jnp.where(sub == k, h, y8)
        y_ref[pl.ds(r0, 8), :] = y8
        return h

    h_scr[...] = lax.fori_loop(0, tb // 8, body, h_scr[...])


def _rglru(z, conv_st, h0, cw, cb, wr, wi, br, bi, lam, *, bsz, t_len, tb, first_pos_is_zero):
    nblk = t_len // tb
    w = LRU_WIDTH
    vec = pl.BlockSpec((1, w), lambda b, j: (0, 0))
    return pl.pallas_call(
        functools.partial(_lru_kernel, tb=tb, first_pos_is_zero=first_pos_is_zero),
        grid=(bsz, nblk),
        in_specs=[
            pl.BlockSpec((tb, w), lambda b, j: (b * nblk + j, UA_OFF // w)),
            pl.BlockSpec((1, CONV_W - 1, w), lambda b, j: (b, 0, 0)),
            pl.BlockSpec((1, 1, w), lambda b, j: (b, 0, 0)),
            pl.BlockSpec((CONV_W, w), lambda b, j: (0, 0)),
            vec,
            pl.BlockSpec((w, w), lambda b, j: (0, 0)),
            pl.BlockSpec((w, w), lambda b, j: (0, 0)),
            vec, vec, vec,
        ],
        out_specs=pl.BlockSpec((tb, w), lambda b, j: (b * nblk + j, 0)),
        out_shape=jax.ShapeDtypeStruct((bsz * t_len, w), F32),
        scratch_shapes=[pltpu.VMEM((tb + 8, w), F32), pltpu.VMEM((tb, w), F32),
                        pltpu.VMEM((tb, w), F32), pltpu.VMEM((1, w), F32)],
        compiler_params=_cparams(("parallel", "arbitrary")),
        name="rglru",
    )(z, conv_st, h0, cw, cb, wr, wi, br, bi, lam)


def _hgrn_kernel(q_ref, f_ref, i_ref, g_ref, lbraw_ref, ng_ref, s0_ref, y_ref, sfin_ref,
                 st_scr, b_scr, k_scr, *, layer, nch, nv):
    c_rows = HG_CHUNK
    nsub = -(-nv // HG_SUB)
    j = pl.program_id(2)

    @pl.when(j == 0)
    def _():
        st_scr[...] = s0_ref[0, 0].T

    raw = lbraw_ref[...]
    e = jnp.exp(raw - jnp.max(raw, axis=0, keepdims=True))
    soft = e / jnp.sum(e, axis=0, keepdims=True)
    lb = jnp.sum(soft[0:layer + 1], axis=0, keepdims=True) - soft[0:1]
    ng = ng_ref[...]

    row = lax.broadcasted_iota(jnp.int32, (c_rows, HG_D), 0)
    col = lax.broadcasted_iota(jnp.int32, (c_rows, HG_D), 1)
    tri = (row >= col).astype(BF16)
    sub_row = lax.broadcasted_iota(jnp.int32, (HG_SUB, HG_D), 0)
    sub_col = lax.broadcasted_iota(jnp.int32, (HG_SUB, HG_D), 1)

    def load(ref, r0):
        x = ref[pl.ds(r0, nv), :]
        if nv < c_rows:
            x = jnp.concatenate([x, jnp.zeros((c_rows - nv, HG_D), F32)], axis=0)
        return x

    def chunk(c, carry):
        r0 = pl.multiple_of(c * nv, nv)
        qz = load(q_ref, r0)
        fz = load(f_ref, r0)
        v = load(i_ref, r0)
        gz = load(g_ref, r0)
        q = qz * _sigmoid(qz)
        logf = jnp.log(lb + (1.0 - lb) * _sigmoid(fz))
        k = (1.0 - lb) * _sigmoid(-fz)
        if nv < c_rows:
            logf = jnp.where(row < nv, logf, 0.0)
            k = jnp.where(row < nv, k, 0.0)
        hi = logf.astype(BF16)
        r1 = logf - hi.astype(F32)
        mid = r1.astype(BF16)
        lo = (r1 - mid.astype(F32)).astype(BF16)
        b = _dot(tri, hi) + _dot(tri, mid) + _dot(tri, lo)
        b_scr[...] = b
        k_scr[...] = k

        st = st_scr[...]
        o = _dot_nt((q * jnp.exp(b)).astype(BF16), st.astype(BF16))
        bl = b[c_rows - 1:c_rows, :]
        kd = k * jnp.exp(bl - b)
        st_scr[...] = st * jnp.exp(bl) + _dot_tn(v.astype(BF16), kd.astype(BF16))

        blocks = []
        for sb in range(nsub):
            lo_r = sb * HG_SUB
            q_i = q[lo_r:lo_r + HG_SUB]
            b_i = b[lo_r:lo_r + HG_SUB]
            if sb > 0:
                beta = b_scr[lo_r - 1:lo_r, :]
                qs = q_i * jnp.exp(b_i - beta)
                ks = jnp.where(row < lo_r, k * jnp.exp(jnp.minimum(beta - b, 0.0)), 0.0)
                a_i = _dot_nt(qs.astype(BF16), ks.astype(BF16))
            else:
                a_i = jnp.zeros((HG_SUB, c_rows), F32)
            for s in range(HG_SUB):
                b_s = b_scr[lo_r + s:lo_r + s + 1, :]
                k_s = k_scr[lo_r + s:lo_r + s + 1, :]
                p = q_i * k_s * jnp.exp(jnp.minimum(b_i - b_s, 0.0))
                colsum = jnp.sum(p, axis=-1, keepdims=True)
                a_i = jnp.where((sub_col == lo_r + s) & (sub_row >= s), colsum, a_i)
            blocks.append(a_i)
        if nsub * HG_SUB < c_rows:
            blocks.append(jnp.zeros((c_rows - nsub * HG_SUB, c_rows), F32))
        attn = jnp.concatenate(blocks, axis=0)
        o = o + _dot(attn.astype(BF16), v.astype(BF16))

        y = _rms(o, ng) * (gz * _sigmoid(gz))
        y_ref[pl.ds(r0, nv), :] = y[0:nv]
        return carry

    lax.fori_loop(0, nch, chunk, 0)

    @pl.when(j == pl.num_programs(2) - 1)
    def _():
        sfin_ref[0, 0] = st_scr[...].T


def _hgrn2(z, lb_raw, norm_g, s0, *, layer, bsz, t_len, rb):
    nblk = t_len // rb
    nv = min(rb, HG_CHUNK)
    nch = rb // nv

    def zspec(off):
        return pl.BlockSpec((rb, HG_D), lambda b, h, j: (b * nblk + j, off // HG_D + h))

    return pl.pallas_call(
        functools.partial(_hgrn_kernel, layer=layer, nch=nch, nv=nv),
        grid=(bsz, HG_HEADS, nblk),
        in_specs=[
            zspec(QB_OFF), zspec(FB_OFF), zspec(IB_OFF), zspec(GB_OFF),
            pl.BlockSpec((lb_raw.shape[0], HG_D), lambda b, h, j: (0, h)),
            pl.BlockSpec((1, HG_D), lambda b, h, j: (0, 0)),
            pl.BlockSpec((1, 1, HG_D, HG_D), lambda b, h, j: (b, h, 0, 0)),
        ],
        out_specs=[
            pl.BlockSpec((rb, HG_D), lambda b, h, j: (b * nblk + j, h)),
            pl.BlockSpec((1, 1, HG_D, HG_D), lambda b, h, j: (b, h, 0, 0)),
        ],
        out_shape=[jax.ShapeDtypeStruct((bsz * t_len, HG_HEADS * HG_D), F32),
                   jax.ShapeDtypeStruct((bsz, HG_HEADS, HG_D, HG_D), F32)],
        scratch_shapes=[pltpu.VMEM((HG_D, HG_D), F32), pltpu.VMEM((HG_CHUNK, HG_D), F32),
                        pltpu.VMEM((HG_CHUNK, HG_D), F32)],
        compiler_params=_cparams(("parallel", "parallel", "arbitrary")),
        name="hgrn2",
    )(z, z, z, z, lb_raw, norm_g, s0)


def _head_masks(shape):
    col = lax.broadcasted_iota(jnp.int32, shape, len(shape) - 1)
    return [(col >= h * DIL_DH) & (col < (h + 1) * DIL_DH) for h in range(DIL_HPG)]


def _dil_prompt_kernel(*refs, dil):
    nh = DIL_GW // V7X_LANES
    q_refs, kp_refs, kc_refs, vp_refs, vc_refs = (refs[c * nh:(c + 1) * nh] for c in range(5))
    o_ref, l_ref = refs[5 * nh], refs[5 * nh + 1]
    o_scr, l_scr = refs[5 * nh + 2:5 * nh + 2 + nh], refs[5 * nh + 2 + nh:]
    i = pl.program_id(0)
    qi = lax.broadcasted_iota(jnp.int32, (DIL_QB, 2 * DIL_QB), 0)
    kj = lax.broadcasted_iota(jnp.int32, (DIL_QB, 2 * DIL_QB), 1) - DIL_QB
    dist = qi - kj
    first_key = jnp.where(i > 0, -DIL_QB, 0)
    valid = (dist >= 0) & (dist <= DIL_QB) & (kj >= first_key)
    col = lax.broadcasted_iota(jnp.int32, (DIL_QB, V7X_LANES), 1)
    masks = [(col >= h * DIL_DH) & (col < (h + 1) * DIL_DH) for h in range(V7X_LANES // DIL_DH)]

    def sub(ref, r):
        if dil == 1:
            return ref[...]
        return ref[pl.ds(r, DIL_QB, stride=dil), :]

    for r in range(dil):
        for c in range(nh):
            q = sub(q_refs[c], r) * (DIL_DH ** -0.5)
            kk = jnp.concatenate([sub(kp_refs[c], r), sub(kc_refs[c], r)], axis=0).astype(BF16)
            vv = jnp.concatenate([sub(vp_refs[c], r), sub(vc_refs[c], r)], axis=0).astype(BF16)
            o_acc = jnp.zeros((DIL_QB, V7X_LANES), F32)
            l_acc = jnp.zeros((DIL_QB, V7X_LANES), F32)
            for h, mask in enumerate(masks):
                qh = jnp.where(mask, q, 0.0).astype(BF16)
                s = jnp.where(valid, _dot_nt(qh, kk), NEG)
                m = jnp.max(s, axis=-1, keepdims=True)
                p = jnp.where(valid, jnp.exp(s - m), 0.0)
                l = jnp.sum(p, axis=-1, keepdims=True)
                oh = _dot(p.astype(BF16), vv) * (1.0 / l)
                o_acc = jnp.where(mask, oh, o_acc)
                l_acc = jnp.where(mask, m + jnp.log(l), l_acc)
            if dil == 1:
                o_scr[c][...] = o_acc
                l_scr[c][...] = l_acc
            else:
                o_scr[c][pl.ds(r, DIL_QB, stride=dil), :] = o_acc
                l_scr[c][pl.ds(r, DIL_QB, stride=dil), :] = l_acc
    for c in range(nh):
        o_ref[:, c * V7X_LANES:(c + 1) * V7X_LANES] = o_scr[c][...]
        l_ref[:, c * V7X_LANES:(c + 1) * V7X_LANES] = l_scr[c][...]


def _dilated_prompt(z, g, t_len):
    _, dil = DIL_GROUPS[g]
    span = dil * DIL_QB
    nb = t_len // span
    nh = DIL_GW // V7X_LANES
    qc, kc, vc = ((off + g * DIL_GW) // V7X_LANES for off in (QC_OFF, KC_OFF, VC_OFF))
    cur = lambda c0: [pl.BlockSpec((span, V7X_LANES), lambda i, c=c0 + k: (i, c)) for k in range(nh)]
    prev = lambda c0: [pl.BlockSpec((span, V7X_LANES), lambda i, c=c0 + k: (jnp.maximum(i - 1, 0), c))
                       for k in range(nh)]
    out = pl.BlockSpec((span, DIL_GW), lambda i: (i, 0))
    return pl.pallas_call(
        functools.partial(_dil_prompt_kernel, dil=dil),
        grid=(nb,),
        in_specs=cur(qc) + prev(kc) + cur(kc) + prev(vc) + cur(vc),
        out_specs=[out, out],
        out_shape=[jax.ShapeDtypeStruct((t_len, DIL_GW), F32)] * 2,
        scratch_shapes=[pltpu.VMEM((span, V7X_LANES), F32)] * (2 * nh),
        compiler_params=_cparams(("parallel",)),
        name=f"dil_prompt_{g}",
    )(*([z] * (5 * nh)))


def _dil_decode_kernel(q_ref, kn_ref, vn_ref, kvt_ref, nt_ref, o_ref, l_ref, co_ref, *, dil, wc, t_len):
    nq = DIL_HPG * t_len
    q = q_ref[...] * (DIL_DH ** -0.5)
    masks = _head_masks((t_len, DIL_GW))
    q4 = jnp.concatenate([jnp.where(masks[h], q, 0.0) for h in range(DIL_HPG)], axis=0).astype(BF16)
    pad = jnp.zeros((V7X_LANES - t_len, DIL_GW), F32)
    kn = jnp.concatenate([kn_ref[...], pad], axis=0).astype(BF16)
    vn = jnp.concatenate([vn_ref[...], pad], axis=0).astype(BF16)
    kt = kvt_ref[0, 0, 0:DIL_GW, :]
    vt = kvt_ref[0, 0, DIL_GW:2 * DIL_GW, :]

    s_c = _dot(q4, kt.astype(BF16))
    s_n = _dot_nt(q4, kn)
    t_c = lax.broadcasted_iota(jnp.int32, (nq, wc), 0) & (t_len - 1)
    p_c = lax.broadcasted_iota(jnp.int32, (nq, wc), 1)
    valid_c = (((wc + t_c - p_c) & (dil - 1)) == 0) & (p_c >= t_c)
    t_n = lax.broadcasted_iota(jnp.int32, (nq, V7X_LANES), 0) & (t_len - 1)
    u_n = lax.broadcasted_iota(jnp.int32, (nq, V7X_LANES), 1)
    valid_n = (u_n <= t_n) & (((t_n - u_n) & (dil - 1)) == 0)
    s_c = jnp.where(valid_c, s_c, NEG)
    s_n = jnp.where(valid_n, s_n, NEG)
    m = jnp.maximum(jnp.max(s_c, axis=-1, keepdims=True), jnp.max(s_n, axis=-1, keepdims=True))
    e_c = jnp.where(valid_c, jnp.exp(s_c - m), 0.0)
    e_n = jnp.where(valid_n, jnp.exp(s_n - m), 0.0)
    l = jnp.sum(e_c, axis=-1, keepdims=True) + jnp.sum(e_n, axis=-1, keepdims=True)
    o4 = (_dot_nt(e_c.astype(BF16), vt.astype(BF16)) + _dot(e_n.astype(BF16), vn)) * (1.0 / l)
    lse4 = m + jnp.log(l)
    o = jnp.zeros((t_len, DIL_GW), F32)
    lse = jnp.zeros((t_len, DIL_GW), F32)
    for h in range(DIL_HPG):
        o = jnp.where(masks[h], o4[h * t_len:(h + 1) * t_len], o)
        lse = jnp.where(masks[h], lse4[h * t_len:(h + 1) * t_len], lse)
    o_ref[...] = o
    l_ref[...] = lse

    rolled = pltpu.roll(kvt_ref[0, 0], wc - t_len, axis=1)
    co_ref[0, 0] = rolled
    lane = lax.broadcasted_iota(jnp.int32, (2 * DIL_GW, V7X_LANES), 1)
    co_ref[0, 0, :, wc - V7X_LANES:wc] = jnp.where(lane >= V7X_LANES - t_len, nt_ref[0, 0],
                                                   rolled[:, wc - V7X_LANES:wc])


def _dilated_decode(z, kvt, nt, g, layer, bsz, t_len):
    _, dil = DIL_GROUPS[g]
    wc = kvt.shape[-1]
    qc, kc, vc = (QC_OFF // DIL_GW + g, KC_OFF // DIL_GW + g, VC_OFF // DIL_GW + g)
    zs = lambda c: pl.BlockSpec((t_len, DIL_GW), lambda b: (b, c))
    out = pl.BlockSpec((t_len, DIL_GW), lambda b: (b, 0))
    return pl.pallas_call(
        functools.partial(_dil_decode_kernel, dil=dil, wc=wc, t_len=t_len),
        grid=(bsz,),
        in_specs=[zs(qc), zs(kc), zs(vc),
                  pl.BlockSpec((1, 1, 2 * DIL_GW, wc), lambda b: (layer, b, 0, 0)),
                  pl.BlockSpec((1, 1, 2 * DIL_GW, V7X_LANES), lambda b: (0, b, 0, 0))],
        out_specs=[out, out, pl.BlockSpec((1, 1, 2 * DIL_GW, wc), lambda b: (0, b, 0, 0))],
        out_shape=[jax.ShapeDtypeStruct((bsz * t_len, DIL_GW), F32)] * 2
        + [jax.ShapeDtypeStruct((1, bsz, 2 * DIL_GW, wc), F32)],
        compiler_params=_cparams(("parallel",)),
        name=f"dil_decode_{g}",
    )(z, z, z, kvt, nt)


def _mix_kernel(ya_ref, yb_ref, o0_ref, o1_ref, o2_ref, l0_ref, l1_ref, l2_ref, gate_ref, bg_ref, x_ref,
                wa_ref, wb_ref, wc_ref, wo_ref, npost_ref, npre_ref, x1_ref, hm_ref):
    l0, l1, l2 = l0_ref[...], l1_ref[...], l2_ref[...]
    m = jnp.maximum(jnp.maximum(l0, l1), l2)
    e0, e1, e2 = jnp.exp(l0 - m), jnp.exp(l1 - m), jnp.exp(l2 - m)
    yc = (e0 * o0_ref[...] + e1 * o1_ref[...] + e2 * o2_ref[...]) * (1.0 / (e0 + e1 + e2))
    gates = _sigmoid(gate_ref[...] + bg_ref[...])
    mixed = (gates[:, 0:D_MODEL] * _dot(ya_ref[...].astype(BF16), wa_ref[...])
             + gates[:, D_MODEL:2 * D_MODEL] * _dot(yb_ref[...].astype(BF16), wb_ref[...])
             + gates[:, 2 * D_MODEL:3 * D_MODEL] * _dot(yc.astype(BF16), wc_ref[...]))
    t = _dot(mixed.astype(BF16), wo_ref[...])
    x1 = x_ref[...] + _rms(t, npost_ref[...])
    x1_ref[...] = x1
    hm_ref[...] = _rms(x1, npre_ref[...]).astype(BF16)


def _mix(ya, yb, os_, ls_, z, bg, x, wa, wb, wc, wo, npost, npre, tm):
    m = x.shape[0]
    rows = lambda w: pl.BlockSpec((tm, w), lambda i: (i, 0))
    full = lambda a: pl.BlockSpec(a.shape, lambda i: (0,) * a.ndim)
    return pl.pallas_call(
        _mix_kernel,
        grid=(m // tm,),
        in_specs=[rows(LRU_WIDTH), rows(512)] + [rows(DIL_GW)] * 6
        + [rows(GATE_W), full(bg), rows(D_MODEL), full(wa), full(wb), full(wc), full(wo), full(npost), full(npre)],
        out_specs=[rows(D_MODEL), rows(D_MODEL)],
        out_shape=[jax.ShapeDtypeStruct((m, D_MODEL), F32), jax.ShapeDtypeStruct((m, D_MODEL), BF16)],
        compiler_params=_cparams(("parallel",)),
        name="mix",
    )(ya, yb, *os_, *ls_, z, bg, x, wa, wb, wc, wo, npost, npre)


def _mlp_kernel(hm_ref, x1_ref, wu_ref, wd_ref, npost_ref, x2_ref, acc_scr):
    k = pl.program_id(1)

    @pl.when(k == 0)
    def _():
        acc_scr[...] = jnp.zeros_like(acc_scr)

    up = jnp.maximum(_dot(hm_ref[...], wu_ref[...]), 0.0)
    acc_scr[...] += _dot((up * up).astype(BF16), wd_ref[...])

    @pl.when(k == pl.num_programs(1) - 1)
    def _():
        x2_ref[...] = x1_ref[...] + _rms(acc_scr[...], npost_ref[...])


def _mlp(hm, x1, wu, wd, npost, tm, tf):
    m = x1.shape[0]
    return pl.pallas_call(
        _mlp_kernel,
        grid=(m // tm, D_FF // tf),
        in_specs=[
            pl.BlockSpec((tm, D_MODEL), lambda i, k: (i, 0)),
            pl.BlockSpec((tm, D_MODEL), lambda i, k: (i, 0)),
            pl.BlockSpec((D_MODEL, tf), lambda i, k: (0, k)),
            pl.BlockSpec((tf, D_MODEL), lambda i, k: (k, 0)),
            pl.BlockSpec((1, D_MODEL), lambda i, k: (0, 0)),
        ],
        out_specs=pl.BlockSpec((tm, D_MODEL), lambda i, k: (i, 0)),
        out_shape=jax.ShapeDtypeStruct((m, D_MODEL), F32),
        scratch_shapes=[pltpu.VMEM((tm, D_MODEL), F32)],
        compiler_params=_cparams(("parallel", "arbitrary")),
        name="mlp",
    )(hm, x1, wu, wd, npost)


def _permute_w_in(w):
    d = w.shape[0]
    n_rest = N_IN - GATE_W
    return jnp.concatenate([w[:, n_rest:], w[:, :n_rest], jnp.zeros((d, N_PAD - N_IN), w.dtype)],
                           axis=1).astype(BF16)


def _block_diag(w):
    n, d, _ = w.shape
    eye = jnp.eye(n, dtype=w.dtype)
    return jnp.einsum("nde,nm->ndme", w, eye).reshape(n * d, n * d).astype(BF16)


def _row(v):
    return v.reshape(1, -1)


def _layer(x, lw, layer, *, bsz, t_len, conv_st, h0, s0, caches, tiles):
    m = bsz * t_len
    z = _inproj(x, lw["n_mix_pre"], lw["w_in"], tiles["a_tm"], tiles["a_tn"])
    ya = _rglru(z, conv_st, h0, lw["conv_w"], lw["conv_b"], lw["wr"], lw["wi"], lw["br"], lw["bi"],
                lw["lam"], bsz=bsz, t_len=t_len, tb=tiles["lru_tb"], first_pos_is_zero=caches is None)
    yb, s_fin = _hgrn2(z, lw["lb_raw"], lw["hg_norm"], s0, layer=layer, bsz=bsz, t_len=t_len,
                       rb=tiles["hg_rb"])
    os_, ls_, new_caches = [], [], []
    for g in range(len(DIL_GROUPS)):
        if caches is None:
            o, l = _dilated_prompt(z, g, t_len)
        else:
            kvt = caches[g]
            kv_new = jnp.stack([z[:, KC_OFF + g * DIL_GW:KC_OFF + (g + 1) * DIL_GW],
                                z[:, VC_OFF + g * DIL_GW:VC_OFF + (g + 1) * DIL_GW]], axis=1)
            nt = kv_new.reshape(bsz, t_len, 2 * DIL_GW).transpose(0, 2, 1)
            nt = jnp.pad(nt, ((0, 0), (0, 0), (V7X_LANES - t_len, 0)))[None]
            o, l, c_new = _dilated_decode(z, kvt, nt, g, layer, bsz, t_len)
            new_caches.append(c_new)
        os_.append(o)
        ls_.append(l)
    x1, hm = _mix(ya, yb, os_, ls_, z, lw["b_gate"], x, lw["w_br_a"], lw["w_br_b"], lw["w_br_c"],
                  lw["w_out"], lw["n_mix_post"], lw["n_mlp_pre"], tiles["mix_tm"])
    x2 = _mlp(hm, x1, lw["w_up"], lw["w_down"], lw["n_mlp_post"], tiles["mlp_tm"], tiles["mlp_tf"])
    return x2, z, ya, s_fin, new_caches


def kernel(x_prompt, x_sample, state_conv, state_lru, state_hgrn, cache_win128, cache_win512, cache_win2048, norm_mix_pre, norm_mix_post, norm_mlp_pre, norm_mlp_post, w_in, b_gate, conv_w, conv_b, lru_wr, lru_br, lru_wi, lru_bi, lru_lambda, hgrn_lb_raw, hgrn_norm, w_br_a, w_br_b, w_br_c, w_out, w_mlp_up, w_mlp_down):
    depth = w_in.shape[0]
    bp, t_p, _ = x_prompt.shape
    bs, t_s, _ = x_sample.shape
    assert bp == 1 and t_p % (DIL_GROUPS[-1][1] * DIL_QB) == 0 and t_s == V7X_SUBLANES
    caches_in = (cache_win128, cache_win512, cache_win2048)
    caches_t = [c.transpose(0, 1, 3, 4, 5, 2).reshape(depth, bs, 2 * DIL_GW, c.shape[2]) for c in caches_in]

    tiles_p = dict(a_tm=1024, a_tn=1024, lru_tb=512, hg_rb=1024, mix_tm=256, mlp_tm=1024, mlp_tf=1024)
    m_s = bs * t_s
    tiles_s = dict(a_tm=m_s, a_tn=1024, lru_tb=t_s, hg_rb=t_s, mix_tm=m_s, mlp_tm=m_s, mlp_tf=1024)

    xp = x_prompt.reshape(bp * t_p, D_MODEL)
    xs = x_sample.reshape(bs * t_s, D_MODEL)
    outs_p = [[] for _ in range(6)]
    outs_s = [[] for _ in range(6)]
    for l in range(depth):
        lw = dict(
            n_mix_pre=_row(norm_mix_pre[l]), n_mix_post=_row(norm_mix_post[l]),
            n_mlp_pre=_row(norm_mlp_pre[l]), n_mlp_post=_row(norm_mlp_post[l]),
            w_in=_permute_w_in(w_in[l]), b_gate=_row(b_gate[l]), conv_w=conv_w[l], conv_b=_row(conv_b[l]),
            wr=_block_diag(lru_wr[l]), wi=_block_diag(lru_wi[l]), br=_row(lru_br[l]), bi=_row(lru_bi[l]),
            lam=_row(lru_lambda[l]), lb_raw=hgrn_lb_raw, hg_norm=_row(hgrn_norm[l]),
            w_br_a=w_br_a[l].astype(BF16), w_br_b=w_br_b[l].astype(BF16), w_br_c=w_br_c[l].astype(BF16),
            w_out=w_out[l].astype(BF16), w_up=w_mlp_up[l].astype(BF16), w_down=w_mlp_down[l].astype(BF16))

        xp, zp, ya_p, sp, _ = _layer(
            xp, lw, l, bsz=bp, t_len=t_p, conv_st=jnp.zeros((bp, CONV_W - 1, LRU_WIDTH), F32),
            h0=jnp.zeros((bp, 1, LRU_WIDTH), F32), s0=jnp.zeros((bp, HG_HEADS, HG_D, HG_D), F32),
            caches=None, tiles=tiles_p)
        xs, zs, ya_s, ss, cs = _layer(
            xs, lw, l, bsz=bs, t_len=t_s, conv_st=state_conv[l], h0=state_lru[l][:, None, :],
            s0=state_hgrn[l], caches=caches_t, tiles=tiles_s)

        outs_p[0].append(zp[t_p - (CONV_W - 1):, UA_OFF:UA_OFF + LRU_WIDTH][None])
        outs_p[1].append(ya_p[t_p - 1:])
        outs_p[2].append(sp)
        for g, (window, _) in enumerate(DIL_GROUPS):
            w = min(window, t_p)
            kv = jnp.stack([zp[t_p - w:, KC_OFF + g * DIL_GW:KC_OFF + (g + 1) * DIL_GW],
                            zp[t_p - w:, VC_OFF + g * DIL_GW:VC_OFF + (g + 1) * DIL_GW]], axis=1)
            outs_p[3 + g].append(kv.reshape(1, w, 2, DIL_HPG, DIL_DH))
        zs3 = zs.reshape(bs, t_s, N_PAD)
        outs_s[0].append(zs3[:, t_s - (CONV_W - 1):, UA_OFF:UA_OFF + LRU_WIDTH])
        outs_s[1].append(ya_s.reshape(bs, t_s, LRU_WIDTH)[:, t_s - 1])
        outs_s[2].append(ss)
        for g in range(len(DIL_GROUPS)):
            wc = cs[g].shape[-1]
            outs_s[3 + g].append(cs[g].reshape(bs, 2, DIL_HPG, DIL_DH, wc).transpose(0, 4, 1, 2, 3))

    res_p = [jnp.stack(v, axis=0) for v in outs_p]
    res_s = [jnp.stack(v, axis=0) for v in outs_s]
    return (xp.reshape(bp, t_p, D_MODEL), xs.reshape(bs, t_s, D_MODEL),
            res_p[0], res_s[0], res_p[1], res_s[1], res_p[2], res_s[2],
            res_p[3], res_s[3], res_p[4], res_s[4], res_p[5], res_s[5])
```

```python
import functools

import jax
import jax.numpy as jnp
from jax import lax
from jax.experimental import pallas as pl
from jax.experimental.pallas import tpu as pltpu

F32 = jnp.float32
BF16 = jnp.bfloat16

D_MODEL = 1024
LRU_WIDTH = 512
LRU_C = 8.0
CONV_W = 4
HG_HEADS = 4
HG_D = 128
DIL_GROUPS = ((128, 1), (512, 4), (2048, 16))
DIL_HPG = 4
DIL_DH = 64
DIL_GW = DIL_HPG * DIL_DH
DIL_QB = 128
DIL_RB = 2048
N_BRANCH = 3
D_FF = 4096
EPS = 1e-6
NEG = -1e30
LOG2E = 1.4426950408889634

GATE_W = N_BRANCH * D_MODEL
UA_OFF = GATE_W
QB_OFF = UA_OFF + LRU_WIDTH
FB_OFF = QB_OFF + 512
IB_OFF = FB_OFF + 512
GB_OFF = IB_OFF + 512
QC_OFF = GB_OFF + 512
KC_OFF = QC_OFF + 768
VC_OFF = KC_OFF + 768
N_IN = VC_OFF + 768
N_PAD = 8192

V7X_LANES = 128
V7X_SUBLANES = 8
HG_CHUNK = 128
HG_SUB = 16
VMEM_LIMIT = 56 * 1024 * 1024


def _cparams(sem):
    return pltpu.CompilerParams(dimension_semantics=sem, vmem_limit_bytes=VMEM_LIMIT)


def _sigmoid(x):
    return 1.0 / (1.0 + jnp.exp(-x))


def _dot(a, b):
    return jnp.dot(a, b, preferred_element_type=F32)


def _dot_nt(a, b):
    return lax.dot_general(a, b, (((1,), (1,)), ((), ())), preferred_element_type=F32)


def _dot_tn(a, b):
    return lax.dot_general(a, b, (((0,), (0,)), ((), ())), preferred_element_type=F32)


def _rms(x, g):
    ms = jnp.mean(x * x, axis=-1, keepdims=True)
    return x * lax.rsqrt(ms + EPS) * g


def _inproj_kernel(x_ref, g_ref, w_ref, z_ref, h_scr):
    @pl.when(pl.program_id(1) == 0)
    def _():
        h_scr[...] = _rms(x_ref[...], g_ref[...]).astype(BF16)

    z_ref[...] = _dot(h_scr[...], w_ref[...])


def _inproj(x, g, w, tm, tn):
    m = x.shape[0]
    return pl.pallas_call(
        _inproj_kernel,
        grid=(m // tm, N_PAD // tn),
        in_specs=[
            pl.BlockSpec((tm, D_MODEL), lambda i, j: (i, 0)),
            pl.BlockSpec((1, D_MODEL), lambda i, j: (0, 0)),
            pl.BlockSpec((D_MODEL, tn), lambda i, j: (0, j)),
        ],
        out_specs=pl.BlockSpec((tm, tn), lambda i, j: (i, j)),
        out_shape=jax.ShapeDtypeStruct((m, N_PAD), F32),
        scratch_shapes=[pltpu.VMEM((tm, D_MODEL), BF16)],
        compiler_params=_cparams(("parallel", "arbitrary")),
        name="inproj",
    )(x, g, w)


def _lru_kernel(u_ref, cst_ref, h0_ref, cw_ref, cb_ref, wr_ref, wi_ref, br_ref, bi_ref, lam_ref,
                y_ref, ext_scr, a_scr, b_scr, h_scr, *, tb, first_pos_is_zero):
    j = pl.program_id(1)

    @pl.when(j == 0)
    def _():
        ext_scr[0:8, :] = jnp.zeros((8, LRU_WIDTH), F32)
        ext_scr[5:8, :] = cst_ref[0]
        h_scr[...] = h0_ref[0]

    u = u_ref[...]
    ext_scr[8:8 + tb, :] = u
    cw = cw_ref[...]
    xc = (cb_ref[...] + ext_scr[5:5 + tb, :] * cw[0:1] + ext_scr[6:6 + tb, :] * cw[1:2]
          + ext_scr[7:7 + tb, :] * cw[2:3] + u * cw[3:4])
    ext_scr[5:8, :] = u[tb - 3:tb]

    xb = xc.astype(BF16)
    r = _sigmoid(_dot(xb, wr_ref[...]) + br_ref[...])
    i = _sigmoid(_dot(xb, wi_ref[...]) + bi_ref[...])
    nl = -lam_ref[...]
    softplus = jnp.maximum(nl, 0.0) + jnp.log(1.0 + jnp.exp(-jnp.abs(nl)))
    log_a = -LRU_C * r * softplus
    a = jnp.exp(log_a)
    mult = jnp.sqrt(1.0 - jnp.exp(2.0 * log_a))
    if first_pos_is_zero:
        row = lax.broadcasted_iota(jnp.int32, (tb, LRU_WIDTH), 0)
        mult = jnp.where((row == 0) & (j == 0), 1.0, mult)
    a_scr[...] = a
    b_scr[...] = mult * i * xc

    sub = lax.broadcasted_iota(jnp.int32, (8, LRU_WIDTH), 0)

    def body(g, h):
        r0 = pl.multiple_of(g * 8, 8)
        ac = a_scr[pl.ds(r0, 8), :]
        bc = b_scr[pl.ds(r0, 8), :]
        for k in (1, 2, 4):
            keep = sub >= k
            bc = jnp.where(keep, ac * pltpu.roll(bc, k, axis=0) + bc, bc)
            ac = jnp.where(keep, ac * pltpu.roll(ac, k, axis=0), ac)
        y8 = ac * h + bc
        y_ref[pl.ds(r0, 8), :] = y8
        return y8[7:8]

    h_scr[...] = lax.fori_loop(0, tb // 8, body, h_scr[...], unroll=min(8, tb // 8))


def _rglru(z, conv_st, h0, cw, cb, wr, wi, br, bi, lam, *, bsz, t_len, tb, first_pos_is_zero):
    nblk = t_len // tb
    w = LRU_WIDTH
    vec = pl.BlockSpec((1, w), lambda b, j: (0, 0))
    return pl.pallas_call(
        functools.partial(_lru_kernel, tb=tb, first_pos_is_zero=first_pos_is_zero),
        grid=(bsz, nblk),
        in_specs=[
            pl.BlockSpec((tb, w), lambda b, j: (b * nblk + j, UA_OFF // w)),
            pl.BlockSpec((1, CONV_W - 1, w), lambda b, j: (b, 0, 0)),
            pl.BlockSpec((1, 1, w), lambda b, j: (b, 0, 0)),
            pl.BlockSpec((CONV_W, w), lambda b, j: (0, 0)),
            vec,
            pl.BlockSpec((w, w), lambda b, j: (0, 0)),
            pl.BlockSpec((w, w), lambda b, j: (0, 0)),
            vec, vec, vec,
        ],
        out_specs=pl.BlockSpec((tb, w), lambda b, j: (b * nblk + j, 0)),
        out_shape=jax.ShapeDtypeStruct((bsz * t_len, w), F32),
        scratch_shapes=[pltpu.VMEM((tb + 8, w), F32), pltpu.VMEM((tb, w), F32),
                        pltpu.VMEM((tb, w), F32), pltpu.VMEM((1, w), F32)],
        compiler_params=_cparams(("parallel", "arbitrary")),
        name="rglru",
    )(z, conv_st, h0, cw, cb, wr, wi, br, bi, lam)


def _hgrn_kernel(q_ref, f_ref, i_ref, g_ref, lbraw_ref, ng_ref, s0_ref, y_ref, sfin_ref,
                 st_scr, b_scr, k_scr, *, layer, nch, nv):
    c_rows = HG_CHUNK
    nsub = -(-nv // HG_SUB)
    j = pl.program_id(1)

    @pl.when(j == 0)
    def _():
        for h in range(HG_HEADS):
            st_scr[h] = s0_ref[0, h].T

    raw = lbraw_ref[...]
    e = jnp.exp(raw - jnp.max(raw, axis=0, keepdims=True))
    soft = e / jnp.sum(e, axis=0, keepdims=True)
    lb_all = jnp.sum(soft[0:layer + 1], axis=0, keepdims=True) - soft[0:1]
    ng = ng_ref[...]

    row = lax.broadcasted_iota(jnp.int32, (c_rows, HG_D), 0)
    col = lax.broadcasted_iota(jnp.int32, (c_rows, HG_D), 1)
    tri = (row >= col).astype(BF16)
    sub_row = lax.broadcasted_iota(jnp.int32, (HG_SUB, HG_D), 0)
    sub_col = lax.broadcasted_iota(jnp.int32, (HG_SUB, HG_D), 1)

    def load(ref, r0, cs):
        x = ref[pl.ds(r0, nv), cs]
        if nv < c_rows:
            x = jnp.concatenate([x, jnp.zeros((c_rows - nv, HG_D), F32)], axis=0)
        return x

    def head(h, r0):
        cs = slice(h * HG_D, (h + 1) * HG_D)
        lb = lb_all[:, cs]
        qz = load(q_ref, r0, cs)
        fz = load(f_ref, r0, cs)
        v = load(i_ref, r0, cs)
        gz = load(g_ref, r0, cs)
        q = qz * _sigmoid(qz)
        logf = jnp.log(lb + (1.0 - lb) * _sigmoid(fz))
        k = (1.0 - lb) * _sigmoid(-fz)
        if nv < c_rows:
            logf = jnp.where(row < nv, logf, 0.0)
            k = jnp.where(row < nv, k, 0.0)
        hi = logf.astype(BF16)
        r1 = logf - hi.astype(F32)
        mid = r1.astype(BF16)
        lo = (r1 - mid.astype(F32)).astype(BF16)
        b2 = (_dot(tri, hi) + _dot(tri, mid) + _dot(tri, lo)) * LOG2E
        b_scr[h] = b2
        k_scr[h] = k

        st = st_scr[h]
        o = _dot_nt((q * jnp.exp2(b2)).astype(BF16), st.astype(BF16))
        bl = b2[c_rows - 1:c_rows, :]
        kd = k * jnp.exp2(bl - b2)
        st_scr[h] = st * jnp.exp2(bl) + _dot_tn(v.astype(BF16), kd.astype(BF16))

        blocks = []
        for sb in range(nsub):
            lo_r = sb * HG_SUB
            q_i = q[lo_r:lo_r + HG_SUB]
            b_i = b2[lo_r:lo_r + HG_SUB]
            if sb > 0:
                beta = b_scr[h, lo_r - 1:lo_r, :]
                qs = q_i * jnp.exp2(b_i - beta)
                ks = jnp.where(row < lo_r, k * jnp.exp2(beta - b2), 0.0)
                a_off = _dot_nt(qs.astype(BF16), ks.astype(BF16))
            else:
                a_off = jnp.zeros((HG_SUB, c_rows), F32)
            a_dia = jnp.zeros((HG_SUB, c_rows), F32)
            for s in range(HG_SUB):
                b_s = b_scr[h, lo_r + s:lo_r + s + 1, :]
                k_s = k_scr[h, lo_r + s:lo_r + s + 1, :]
                p = q_i * k_s * jnp.exp2(b_i - b_s)
                a_dia = jnp.where(sub_col == lo_r + s, jnp.sum(p, axis=-1, keepdims=True), a_dia)
            in_block = (sub_col >= lo_r) & (sub_col <= lo_r + sub_row)
            blocks.append(jnp.where(in_block, a_dia, a_off))
        if nsub * HG_SUB < c_rows:
            blocks.append(jnp.zeros((c_rows - nsub * HG_SUB, c_rows), F32))
        attn = jnp.concatenate(blocks, axis=0)
        o = o + _dot(attn.astype(BF16), v.astype(BF16))

        y = _rms(o, ng) * (gz * _sigmoid(gz))
        y_ref[pl.ds(r0, nv), cs] = y[0:nv]

    def chunk(c, carry):
        r0 = pl.multiple_of(c * nv, nv)
        for h in range(HG_HEADS):
            head(h, r0)
        return carry

    lax.fori_loop(0, nch, chunk, 0)

    @pl.when(j == pl.num_programs(1) - 1)
    def _():
        for h in range(HG_HEADS):
            sfin_ref[0, h] = st_scr[h].T


def _hgrn2(z, lb_raw, norm_g, s0, *, layer, bsz, t_len, rb):
    nblk = t_len // rb
    nv = min(rb, HG_CHUNK)
    nch = rb // nv
    hw = HG_HEADS * HG_D

    def zspec(off):
        return pl.BlockSpec((rb, hw), lambda b, j: (b * nblk + j, off // hw))

    state = pl.BlockSpec((1, HG_HEADS, HG_D, HG_D), lambda b, j: (b, 0, 0, 0))
    return pl.pallas_call(
        functools.partial(_hgrn_kernel, layer=layer, nch=nch, nv=nv),
        grid=(bsz, nblk),
        in_specs=[
            zspec(QB_OFF), zspec(FB_OFF), zspec(IB_OFF), zspec(GB_OFF),
            pl.BlockSpec((lb_raw.shape[0], hw), lambda b, j: (0, 0)),
            pl.BlockSpec((1, HG_D), lambda b, j: (0, 0)),
            state,
        ],
        out_specs=[pl.BlockSpec((rb, hw), lambda b, j: (b * nblk + j, 0)), state],
        out_shape=[jax.ShapeDtypeStruct((bsz * t_len, hw), F32),
                   jax.ShapeDtypeStruct((bsz, HG_HEADS, HG_D, HG_D), F32)],
        scratch_shapes=[pltpu.VMEM((HG_HEADS, HG_D, HG_D), F32),
                        pltpu.VMEM((HG_HEADS, HG_CHUNK, HG_D), F32),
                        pltpu.VMEM((HG_HEADS, HG_CHUNK, HG_D), F32)],
        compiler_params=_cparams(("parallel", "arbitrary")),
        name="hgrn2",
    )(z, z, z, z, lb_raw, norm_g, s0)


def _head_masks(shape):
    col = lax.broadcasted_iota(jnp.int32, shape, len(shape) - 1)
    return [(col >= h * DIL_DH) & (col < (h + 1) * DIL_DH) for h in range(DIL_HPG)]


def _dil_prompt_kernel(*refs, dil):
    nh = DIL_GW // V7X_LANES
    span = dil * DIL_QB
    nsp = DIL_RB // span
    q_refs, kc_refs, vc_refs, kp_refs, vp_refs = (refs[c * nh:(c + 1) * nh] for c in range(5))
    o_ref, l_ref = refs[5 * nh], refs[5 * nh + 1]
    o_scr, l_scr = refs[5 * nh + 2:5 * nh + 2 + nh], refs[5 * nh + 2 + nh:]
    i = pl.program_id(0)
    qi = lax.broadcasted_iota(jnp.int32, (DIL_QB, 2 * DIL_QB), 0)
    kj = lax.broadcasted_iota(jnp.int32, (DIL_QB, 2 * DIL_QB), 1) - DIL_QB
    dist = qi - kj
    band = (dist >= 0) & (dist <= DIL_QB)
    first_key = jnp.where(i > 0, -DIL_QB, 0)
    band_first = band & (kj >= first_key)
    col = lax.broadcasted_iota(jnp.int32, (DIL_QB, V7X_LANES), 1)
    masks = [(col >= h * DIL_DH) & (col < (h + 1) * DIL_DH) for h in range(V7X_LANES // DIL_DH)]

    def sub(ref, n, r):
        if dil == 1:
            return ref[n * DIL_QB:(n + 1) * DIL_QB, :]
        return ref[pl.ds(n * span + r, DIL_QB, stride=dil), :]

    for n in range(nsp):
        valid = band_first if n == 0 else band
        for r in range(dil):
            for c in range(nh):
                q = sub(q_refs[c], n, r) * (DIL_DH ** -0.5)
                k_prev = sub(kp_refs[c], 0, r) if n == 0 else sub(kc_refs[c], n - 1, r)
                v_prev = sub(vp_refs[c], 0, r) if n == 0 else sub(vc_refs[c], n - 1, r)
                kk = jnp.concatenate([k_prev, sub(kc_refs[c], n, r)], axis=0).astype(BF16)
                vv = jnp.concatenate([v_prev, sub(vc_refs[c], n, r)], axis=0).astype(BF16)
                o_acc = jnp.zeros((DIL_QB, V7X_LANES), F32)
                l_acc = jnp.zeros((DIL_QB, V7X_LANES), F32)
                for mask in masks:
                    qh = jnp.where(mask, q, 0.0).astype(BF16)
                    s = jnp.where(valid, _dot_nt(qh, kk), NEG)
                    m = jnp.max(s, axis=-1, keepdims=True)
                    p = jnp.where(valid, jnp.exp(s - m), 0.0)
                    l = jnp.sum(p, axis=-1, keepdims=True)
                    oh = _dot(p.astype(BF16), vv) * (1.0 / l)
                    o_acc = jnp.where(mask, oh, o_acc)
                    l_acc = jnp.where(mask, m + jnp.log(l), l_acc)
                if dil == 1:
                    o_ref[n * DIL_QB:(n + 1) * DIL_QB, c * V7X_LANES:(c + 1) * V7X_LANES] = o_acc
                    l_ref[n * DIL_QB:(n + 1) * DIL_QB, c * V7X_LANES:(c + 1) * V7X_LANES] = l_acc
                else:
                    o_scr[c][pl.ds(n * span + r, DIL_QB, stride=dil), :] = o_acc
                    l_scr[c][pl.ds(n * span + r, DIL_QB, stride=dil), :] = l_acc
    if dil > 1:
        for c in range(nh):
            o_ref[:, c * V7X_LANES:(c + 1) * V7X_LANES] = o_scr[c][...]
            l_ref[:, c * V7X_LANES:(c + 1) * V7X_LANES] = l_scr[c][...]


def _dilated_prompt(z, g, t_len):
    _, dil = DIL_GROUPS[g]
    span = dil * DIL_QB
    nsp = DIL_RB // span
    nb = t_len // DIL_RB
    nh = DIL_GW // V7X_LANES
    qc, kc, vc = ((off + g * DIL_GW) // V7X_LANES for off in (QC_OFF, KC_OFF, VC_OFF))
    cur = lambda c0: [pl.BlockSpec((DIL_RB, V7X_LANES), lambda i, c=c0 + k: (i, c)) for k in range(nh)]
    prev = lambda c0: [pl.BlockSpec((span, V7X_LANES), lambda i, c=c0 + k: (jnp.maximum(i * nsp - 1, 0), c))
                       for k in range(nh)]
    out = pl.BlockSpec((DIL_RB, DIL_GW), lambda i: (i, 0))
    scr = [pltpu.VMEM((DIL_RB, V7X_LANES), F32)] * (2 * nh) if dil > 1 else [pltpu.VMEM((8, V7X_LANES), F32)] * (2 * nh)
    return pl.pallas_call(
        functools.partial(_dil_prompt_kernel, dil=dil),
        grid=(nb,),
        in_specs=cur(qc) + cur(kc) + cur(vc) + prev(kc) + prev(vc),
        out_specs=[out, out],
        out_shape=[jax.ShapeDtypeStruct((t_len, DIL_GW), F32)] * 2,
        scratch_shapes=scr,
        compiler_params=_cparams(("parallel",)),
        name=f"dil_prompt_{g}",
    )(*([z] * (5 * nh)))


def _dil_decode_kernel(q_ref, kn_ref, vn_ref, kvt_ref, nt_ref, *rest, dil, wc, t_len):
    o_ref, l_ref, co_ref = rest[-3:]
    nq = DIL_HPG * t_len
    q = q_ref[...] * (DIL_DH ** -0.5)
    masks = _head_masks((t_len, DIL_GW))
    q4 = jnp.concatenate([jnp.where(masks[h], q, 0.0) for h in range(DIL_HPG)], axis=0).astype(BF16)
    pad = jnp.zeros((V7X_LANES - t_len, DIL_GW), F32)
    kn = jnp.concatenate([kn_ref[...], pad], axis=0).astype(BF16)
    vn = jnp.concatenate([vn_ref[...], pad], axis=0).astype(BF16)
    kt = kvt_ref[0, 0, 0:DIL_GW, :]
    vt = kvt_ref[0, 0, DIL_GW:2 * DIL_GW, :]

    s_c = _dot(q4, kt.astype(BF16))
    s_n = _dot_nt(q4, kn)
    t_c = lax.broadcasted_iota(jnp.int32, (nq, wc), 0) & (t_len - 1)
    p_c = lax.broadcasted_iota(jnp.int32, (nq, wc), 1)
    valid_c = (((wc + t_c - p_c) & (dil - 1)) == 0) & (p_c >= t_c)
    t_n = lax.broadcasted_iota(jnp.int32, (nq, V7X_LANES), 0) & (t_len - 1)
    u_n = lax.broadcasted_iota(jnp.int32, (nq, V7X_LANES), 1)
    valid_n = (u_n <= t_n) & (((t_n - u_n) & (dil - 1)) == 0)
    s_c = jnp.where(valid_c, s_c, NEG)
    s_n = jnp.where(valid_n, s_n, NEG)
    m = jnp.maximum(jnp.max(s_c, axis=-1, keepdims=True), jnp.max(s_n, axis=-1, keepdims=True))
    e_c = jnp.where(valid_c, jnp.exp(s_c - m), 0.0)
    e_n = jnp.where(valid_n, jnp.exp(s_n - m), 0.0)
    l = jnp.sum(e_c, axis=-1, keepdims=True) + jnp.sum(e_n, axis=-1, keepdims=True)
    o4 = (_dot_nt(e_c.astype(BF16), vt.astype(BF16)) + _dot(e_n.astype(BF16), vn)) * (1.0 / l)
    lse4 = m + jnp.log(l)
    o = jnp.zeros((t_len, DIL_GW), F32)
    lse = jnp.zeros((t_len, DIL_GW), F32)
    for h in range(DIL_HPG):
        o = jnp.where(masks[h], o4[h * t_len:(h + 1) * t_len], o)
        lse = jnp.where(masks[h], lse4[h * t_len:(h + 1) * t_len], lse)
    o_ref[...] = o
    l_ref[...] = lse

    rolled = pltpu.roll(kvt_ref[0, 0], wc - t_len, axis=1)
    if wc > V7X_LANES:
        co_ref[0, 0, :, 0:wc - V7X_LANES] = rolled[:, 0:wc - V7X_LANES]
    lane = lax.broadcasted_iota(jnp.int32, (2 * DIL_GW, V7X_LANES), 1)
    co_ref[0, 0, :, wc - V7X_LANES:wc] = jnp.where(lane >= V7X_LANES - t_len, nt_ref[0, 0],
                                                   rolled[:, wc - V7X_LANES:wc])


def _dilated_decode(z, kvt, nt, prev_out, g, layer, bsz, t_len):
    _, dil = DIL_GROUPS[g]
    depth, _, _, wc = kvt.shape
    qc, kc, vc = (QC_OFF // DIL_GW + g, KC_OFF // DIL_GW + g, VC_OFF // DIL_GW + g)
    zs = lambda c: pl.BlockSpec((t_len, DIL_GW), lambda b: (b, c))
    out = pl.BlockSpec((t_len, DIL_GW), lambda b: (b, 0))
    in_specs = [zs(qc), zs(kc), zs(vc),
                pl.BlockSpec((1, 1, 2 * DIL_GW, wc), lambda b: (layer, b, 0, 0)),
                pl.BlockSpec((1, 1, 2 * DIL_GW, V7X_LANES), lambda b: (0, b, 0, 0))]
    args = [z, z, z, kvt, nt]
    aliases = {}
    if prev_out is not None:
        in_specs.append(pl.BlockSpec(memory_space=pl.ANY))
        args.append(prev_out)
        aliases = {len(args) - 1: 2}
    return pl.pallas_call(
        functools.partial(_dil_decode_kernel, dil=dil, wc=wc, t_len=t_len),
        grid=(bsz,),
        in_specs=in_specs,
        out_specs=[out, out, pl.BlockSpec((1, 1, 2 * DIL_GW, wc), lambda b: (layer, b, 0, 0))],
        out_shape=[jax.ShapeDtypeStruct((bsz * t_len, DIL_GW), F32)] * 2
        + [jax.ShapeDtypeStruct((depth, bsz, 2 * DIL_GW, wc), F32)],
        input_output_aliases=aliases,
        compiler_params=_cparams(("parallel",)),
        name=f"dil_decode_{g}",
    )(*args)


def _mix_kernel(ya_ref, yb_ref, o0_ref, o1_ref, o2_ref, l0_ref, l1_ref, l2_ref, gate_ref, bg_ref, x_ref,
                wa_ref, wb_ref, wc_ref, wo_ref, npost_ref, npre_ref, x1_ref, hm_ref):
    l0, l1, l2 = l0_ref[...], l1_ref[...], l2_ref[...]
    m = jnp.maximum(jnp.maximum(l0, l1), l2)
    e0, e1, e2 = jnp.exp(l0 - m), jnp.exp(l1 - m), jnp.exp(l2 - m)
    yc = (e0 * o0_ref[...] + e1 * o1_ref[...] + e2 * o2_ref[...]) * (1.0 / (e0 + e1 + e2))
    gates = _sigmoid(gate_ref[...] + bg_ref[...])
    mixed = (gates[:, 0:D_MODEL] * _dot(ya_ref[...].astype(BF16), wa_ref[...])
             + gates[:, D_MODEL:2 * D_MODEL] * _dot(yb_ref[...].astype(BF16), wb_ref[...])
             + gates[:, 2 * D_MODEL:3 * D_MODEL] * _dot(yc.astype(BF16), wc_ref[...]))
    t = _dot(mixed.astype(BF16), wo_ref[...])
    x1 = x_ref[...] + _rms(t, npost_ref[...])
    x1_ref[...] = x1
    hm_ref[...] = _rms(x1, npre_ref[...]).astype(BF16)


def _mix(ya, yb, os_, ls_, z, bg, x, wa, wb, wc, wo, npost, npre, tm):
    m = x.shape[0]
    rows = lambda w: pl.BlockSpec((tm, w), lambda i: (i, 0))
    full = lambda a: pl.BlockSpec(a.shape, lambda i: (0,) * a.ndim)
    return pl.pallas_call(
        _mix_kernel,
        grid=(m // tm,),
        in_specs=[rows(LRU_WIDTH), rows(512)] + [rows(DIL_GW)] * 6
        + [rows(GATE_W), full(bg), rows(D_MODEL), full(wa), full(wb), full(wc), full(wo), full(npost), full(npre)],
        out_specs=[rows(D_MODEL), rows(D_MODEL)],
        out_shape=[jax.ShapeDtypeStruct((m, D_MODEL), F32), jax.ShapeDtypeStruct((m, D_MODEL), BF16)],
        compiler_params=_cparams(("parallel",)),
        name="mix",
    )(ya, yb, *os_, *ls_, z, bg, x, wa, wb, wc, wo, npost, npre)


def _mlp_kernel(hm_ref, x1_ref, wu_ref, wd_ref, npost_ref, x2_ref, acc_scr):
    k = pl.program_id(1)

    @pl.when(k == 0)
    def _():
        acc_scr[...] = jnp.zeros_like(acc_scr)

    up = jnp.maximum(_dot(hm_ref[...], wu_ref[...]), 0.0)
    acc_scr[...] += _dot((up * up).astype(BF16), wd_ref[...])

    @pl.when(k == pl.num_programs(1) - 1)
    def _():
        x2_ref[...] = x1_ref[...] + _rms(acc_scr[...], npost_ref[...])


def _mlp(hm, x1, wu, wd, npost, tm, tf):
    m = x1.shape[0]
    return pl.pallas_call(
        _mlp_kernel,
        grid=(m // tm, D_FF // tf),
        in_specs=[
            pl.BlockSpec((tm, D_MODEL), lambda i, k: (i, 0)),
            pl.BlockSpec((tm, D_MODEL), lambda i, k: (i, 0)),
            pl.BlockSpec((D_MODEL, tf), lambda i, k: (0, k)),
            pl.BlockSpec((tf, D_MODEL), lambda i, k: (k, 0)),
            pl.BlockSpec((1, D_MODEL), lambda i, k: (0, 0)),
        ],
        out_specs=pl.BlockSpec((tm, D_MODEL), lambda i, k: (i, 0)),
        out_shape=jax.ShapeDtypeStruct((m, D_MODEL), F32),
        scratch_shapes=[pltpu.VMEM((tm, D_MODEL), F32)],
        compiler_params=_cparams(("parallel", "arbitrary")),
        name="mlp",
    )(hm, x1, wu, wd, npost)


def _permute_w_in(w):
    d = w.shape[0]
    n_rest = N_IN - GATE_W
    return jnp.concatenate([w[:, n_rest:], w[:, :n_rest], jnp.zeros((d, N_PAD - N_IN), w.dtype)],
                           axis=1).astype(BF16)


def _block_diag(w):
    n, d, _ = w.shape
    eye = jnp.eye(n, dtype=w.dtype)
    return jnp.einsum("nde,nm->ndme", w, eye).reshape(n * d, n * d).astype(BF16)


def _row(v):
    return v.reshape(1, -1)


def _layer(x, lw, layer, *, bsz, t_len, conv_st, h0, s0, caches, cache_outs, tiles):
    z = _inproj(x, lw["n_mix_pre"], lw["w_in"], tiles["a_tm"], tiles["a_tn"])
    ya = _rglru(z, conv_st, h0, lw["conv_w"], lw["conv_b"], lw["wr"], lw["wi"], lw["br"], lw["bi"],
                lw["lam"], bsz=bsz, t_len=t_len, tb=tiles["lru_tb"], first_pos_is_zero=caches is None)
    yb, s_fin = _hgrn2(z, lw["lb_raw"], lw["hg_norm"], s0, layer=layer, bsz=bsz, t_len=t_len,
                       rb=tiles["hg_rb"])
    os_, ls_, new_caches = [], [], []
    for g in range(len(DIL_GROUPS)):
        if caches is None:
            o, l = _dilated_prompt(z, g, t_len)
        else:
            kv_new = jnp.stack([z[:, KC_OFF + g * DIL_GW:KC_OFF + (g + 1) * DIL_GW],
                                z[:, VC_OFF + g * DIL_GW:VC_OFF + (g + 1) * DIL_GW]], axis=1)
            nt = kv_new.reshape(bsz, t_len, 2 * DIL_GW).transpose(0, 2, 1)
            nt = jnp.pad(nt, ((0, 0), (0, 0), (V7X_LANES - t_len, 0)))[None]
            o, l, c_new = _dilated_decode(z, caches[g], nt, cache_outs[g], g, layer, bsz, t_len)
            new_caches.append(c_new)
        os_.append(o)
        ls_.append(l)
    x1, hm = _mix(ya, yb, os_, ls_, z, lw["b_gate"], x, lw["w_br_a"], lw["w_br_b"], lw["w_br_c"],
                  lw["w_out"], lw["n_mix_post"], lw["n_mlp_pre"], tiles["mix_tm"])
    x2 = _mlp(hm, x1, lw["w_up"], lw["w_down"], lw["n_mlp_post"], tiles["mlp_tm"], tiles["mlp_tf"])
    return x2, z, ya, s_fin, new_caches


def kernel(x_prompt, x_sample, state_conv, state_lru, state_hgrn, cache_win128, cache_win512, cache_win2048, norm_mix_pre, norm_mix_post, norm_mlp_pre, norm_mlp_post, w_in, b_gate, conv_w, conv_b, lru_wr, lru_br, lru_wi, lru_bi, lru_lambda, hgrn_lb_raw, hgrn_norm, w_br_a, w_br_b, w_br_c, w_out, w_mlp_up, w_mlp_down):
    depth = w_in.shape[0]
    bp, t_p, _ = x_prompt.shape
    bs, t_s, _ = x_sample.shape
    assert bp == 1 and t_p % DIL_RB == 0 and t_s == V7X_SUBLANES
    caches_in = (cache_win128, cache_win512, cache_win2048)
    caches_t = [c.transpose(0, 1, 3, 4, 5, 2).reshape(depth, bs, 2 * DIL_GW, c.shape[2]) for c in caches_in]

    m_s = bs * t_s
    tiles_p = dict(a_tm=min(2048, t_p), a_tn=1024, lru_tb=512, hg_rb=1024, mix_tm=256, mlp_tm=1024, mlp_tf=1024)
    tiles_s = dict(a_tm=m_s, a_tn=1024, lru_tb=t_s, hg_rb=t_s, mix_tm=m_s, mlp_tm=m_s, mlp_tf=1024)

    xp = x_prompt.reshape(bp * t_p, D_MODEL)
    xs = x_sample.reshape(bs * t_s, D_MODEL)
    outs_p = [[] for _ in range(6)]
    outs_s = [[] for _ in range(3)]
    cache_outs = [None] * len(DIL_GROUPS)
    for l in range(depth):
        lw = dict(
            n_mix_pre=_row(norm_mix_pre[l]), n_mix_post=_row(norm_mix_post[l]),
            n_mlp_pre=_row(norm_mlp_pre[l]), n_mlp_post=_row(norm_mlp_post[l]),
            w_in=_permute_w_in(w_in[l]), b_gate=_row(b_gate[l]), conv_w=conv_w[l], conv_b=_row(conv_b[l]),
            wr=_block_diag(lru_wr[l]), wi=_block_diag(lru_wi[l]), br=_row(lru_br[l]), bi=_row(lru_bi[l]),
            lam=_row(lru_lambda[l]), lb_raw=hgrn_lb_raw, hg_norm=_row(hgrn_norm[l]),
            w_br_a=w_br_a[l].astype(BF16), w_br_b=w_br_b[l].astype(BF16), w_br_c=w_br_c[l].astype(BF16),
            w_out=w_out[l].astype(BF16), w_up=w_mlp_up[l].astype(BF16), w_down=w_mlp_down[l].astype(BF16))

        xp, zp, ya_p, sp, _ = _layer(
            xp, lw, l, bsz=bp, t_len=t_p, conv_st=jnp.zeros((bp, CONV_W - 1, LRU_WIDTH), F32),
            h0=jnp.zeros((bp, 1, LRU_WIDTH), F32), s0=jnp.zeros((bp, HG_HEADS, HG_D, HG_D), F32),
            caches=None, cache_outs=None, tiles=tiles_p)
        xs, zs, ya_s, ss, cache_outs = _layer(
            xs, lw, l, bsz=bs, t_len=t_s, conv_st=state_conv[l], h0=state_lru[l][:, None, :],
            s0=state_hgrn[l], caches=caches_t, cache_outs=cache_outs, tiles=tiles_s)

        outs_p[0].append(zp[t_p - (CONV_W - 1):, UA_OFF:UA_OFF + LRU_WIDTH][None])
        outs_p[1].append(ya_p[t_p - 1:])
        outs_p[2].append(sp)
        for g, (window, _) in enumerate(DIL_GROUPS):
            w = min(window, t_p)
            kv = jnp.stack([zp[t_p - w:, KC_OFF + g * DIL_GW:KC_OFF + (g + 1) * DIL_GW],
                            zp[t_p - w:, VC_OFF + g * DIL_GW:VC_OFF + (g + 1) * DIL_GW]], axis=1)
            outs_p[3 + g].append(kv.reshape(1, w, 2, DIL_HPG, DIL_DH))
        zs3 = zs.reshape(bs, t_s, N_PAD)
        outs_s[0].append(zs3[:, t_s - (CONV_W - 1):, UA_OFF:UA_OFF + LRU_WIDTH])
        outs_s[1].append(ya_s.reshape(bs, t_s, LRU_WIDTH)[:, t_s - 1])
        outs_s[2].append(ss)

    res_p = [jnp.stack(v, axis=0) for v in outs_p]
    res_s = [jnp.stack(v, axis=0) for v in outs_s]
    win_s = [c.reshape(depth, bs, 2, DIL_HPG, DIL_DH, c.shape[-1]).transpose(0, 1, 5, 2, 3, 4) for c in cache_outs]
    return (xp.reshape(bp, t_p, D_MODEL), xs.reshape(bs, t_s, D_MODEL),
            res_p[0], res_s[0], res_p[1], res_s[1], res_p[2], res_s[2],
            res_p[3], win_s[0], res_p[4], win_s[1], res_p[5], win_s[2])
```

```python
import functools

import jax
import jax.numpy as jnp
from jax import lax
from jax.experimental import pallas as pl
from jax.experimental.pallas import tpu as pltpu

F32 = jnp.float32
BF16 = jnp.bfloat16

D_MODEL = 1024
LRU_WIDTH = 512
LRU_C = 8.0
CONV_W = 4
HG_HEADS = 4
HG_D = 128
HG_W = HG_HEADS * HG_D
DIL_GROUPS = ((128, 1), (512, 4), (2048, 16))
DIL_HPG = 4
DIL_DH = 64
DIL_GW = DIL_HPG * DIL_DH
DIL_W = len(DIL_GROUPS) * DIL_GW
DIL_QB = 128
DIL_RB = 2048
N_BRANCH = 3
D_FF = 4096
EPS = 1e-6
NEG = -1e30
LOG2E = 1.4426950408889634

GATE_W = N_BRANCH * D_MODEL
UA_OFF = GATE_W
QB_OFF = UA_OFF + LRU_WIDTH
IB_OFF = QB_OFF + HG_W
GB_OFF = IB_OFF + HG_W
N_LO = GB_OFF + HG_W
FB_OFF = 0
QC_OFF = FB_OFF + HG_W
KC_OFF = QC_OFF + DIL_W
VC_OFF = KC_OFF + DIL_W
N_HI = 3072
IN_TN = 1024

V7X_LANES = 128
V7X_SUBLANES = 8
HG_CHUNK = 128
HG_SUB = 16
VMEM_LIMIT = 56 * 1024 * 1024


def _cparams(sem):
    return pltpu.CompilerParams(dimension_semantics=sem, vmem_limit_bytes=VMEM_LIMIT)


def _sigmoid(x):
    return 0.5 * jnp.tanh(0.5 * x) + 0.5


def _dot(a, b):
    return jnp.dot(a, b, preferred_element_type=F32)


def _dot_nt(a, b):
    return lax.dot_general(a, b, (((1,), (1,)), ((), ())), preferred_element_type=F32)


def _dot_tn(a, b):
    return lax.dot_general(a, b, (((0,), (0,)), ((), ())), preferred_element_type=F32)


def _rms(x, g):
    ms = jnp.mean(x * x, axis=-1, keepdims=True)
    return x * lax.rsqrt(ms + EPS) * g


def _split3(x):
    hi = x.astype(BF16)
    r1 = x - hi.astype(F32)
    mid = r1.astype(BF16)
    lo = (r1 - mid.astype(F32)).astype(BF16)
    return hi, mid, lo


def _inproj_kernel(x_ref, g_ref, w_ref, lo_ref, hi_ref, h_scr, *, n_lo_blocks):
    j = pl.program_id(1)

    @pl.when(j == 0)
    def _():
        h_scr[...] = _rms(x_ref[...], g_ref[...]).astype(BF16)

    @pl.when(j < n_lo_blocks)
    def _():
        lo_ref[...] = _dot(h_scr[...], w_ref[...]).astype(lo_ref.dtype)

    @pl.when(j >= n_lo_blocks)
    def _():
        hi_ref[...] = _dot(h_scr[...], w_ref[...])


def _inproj(x, g, w, tm, lo_dtype):
    m = x.shape[0]
    tn = IN_TN
    n_lo_blocks = N_LO // tn
    return pl.pallas_call(
        functools.partial(_inproj_kernel, n_lo_blocks=n_lo_blocks),
        grid=(m // tm, (N_LO + N_HI) // tn),
        in_specs=[
            pl.BlockSpec((tm, D_MODEL), lambda i, j: (i, 0)),
            pl.BlockSpec((1, D_MODEL), lambda i, j: (0, 0)),
            pl.BlockSpec((D_MODEL, tn), lambda i, j: (0, j)),
        ],
        out_specs=[pl.BlockSpec((tm, tn), lambda i, j: (i, jnp.minimum(j, n_lo_blocks - 1))),
                   pl.BlockSpec((tm, tn), lambda i, j: (i, jnp.maximum(j - n_lo_blocks, 0)))],
        out_shape=[jax.ShapeDtypeStruct((m, N_LO), lo_dtype), jax.ShapeDtypeStruct((m, N_HI), F32)],
        scratch_shapes=[pltpu.VMEM((tm, D_MODEL), BF16)],
        compiler_params=_cparams(("parallel", "arbitrary")),
        name="inproj",
    )(x, g, w)


def _lru_kernel(u_ref, cst_ref, h0_ref, cw_ref, cb_ref, wr_ref, wi_ref, br_ref, bi_ref, lam_ref,
                y_ref, hlast_ref, ext_scr, a_scr, b_scr, h_scr, *, tb, first_pos_is_zero):
    j = pl.program_id(1)

    @pl.when(j == 0)
    def _():
        ext_scr[0:8, :] = jnp.zeros((8, LRU_WIDTH), F32)
        ext_scr[5:8, :] = cst_ref[0]
        h_scr[...] = h0_ref[0]

    u = u_ref[...].astype(F32)
    ext_scr[8:8 + tb, :] = u
    cw = cw_ref[...]
    xc = (cb_ref[...] + ext_scr[5:5 + tb, :] * cw[0:1] + ext_scr[6:6 + tb, :] * cw[1:2]
          + ext_scr[7:7 + tb, :] * cw[2:3] + u * cw[3:4])
    ext_scr[5:8, :] = u[tb - 3:tb]

    xb = xc.astype(BF16)
    r = _sigmoid(_dot(xb, wr_ref[...]) + br_ref[...])
    i = _sigmoid(_dot(xb, wi_ref[...]) + bi_ref[...])
    nl = -lam_ref[...]
    softplus = jnp.maximum(nl, 0.0) + jnp.log(1.0 + jnp.exp(-jnp.abs(nl)))
    a = jnp.exp(r * (-LRU_C * softplus))
    gap = 1.0 - a * a
    mult = jnp.where(gap > 0.0, gap * lax.rsqrt(gap), 0.0)
    if first_pos_is_zero:
        row = lax.broadcasted_iota(jnp.int32, (tb, LRU_WIDTH), 0)
        mult = jnp.where((row == 0) & (j == 0), 1.0, mult)
    a_scr[...] = a
    b_scr[...] = mult * i * xc

    sub = lax.broadcasted_iota(jnp.int32, (8, LRU_WIDTH), 0)

    def scan8(r0, h):
        ac = a_scr[pl.ds(r0, 8), :]
        bc = b_scr[pl.ds(r0, 8), :]
        for k in (1, 2, 4):
            keep = sub >= k
            bc = jnp.where(keep, ac * pltpu.roll(bc, k, axis=0) + bc, bc)
            ac = jnp.where(keep, ac * pltpu.roll(ac, k, axis=0), ac)
        return ac * h + bc

    groups = 2 if tb % 16 == 0 else 1

    def body(g, h):
        r0 = pl.multiple_of(g * (8 * groups), 8 * groups)
        ys = []
        for k in range(groups):
            ys.append(scan8(r0 + 8 * k, h))
            h = ys[-1][7:8]
        y_ref[pl.ds(r0, 8 * groups), :] = jnp.concatenate(ys, axis=0).astype(y_ref.dtype)
        return h

    n_iter = tb // (8 * groups)
    h_fin = lax.fori_loop(0, n_iter, body, h_scr[...], unroll=min(4, n_iter))
    h_scr[...] = h_fin

    @pl.when(j == pl.num_programs(1) - 1)
    def _():
        hlast_ref[0] = h_fin


def _rglru(z_lo, conv_st, h0, cw, cb, wr, wi, br, bi, lam, *, bsz, t_len, tb, first_pos_is_zero, y_dtype):
    nblk = t_len // tb
    w = LRU_WIDTH
    vec = pl.BlockSpec((1, w), lambda b, j: (0, 0))
    per_seq = pl.BlockSpec((1, 1, w), lambda b, j: (b, 0, 0))
    return pl.pallas_call(
        functools.partial(_lru_kernel, tb=tb, first_pos_is_zero=first_pos_is_zero),
        grid=(bsz, nblk),
        in_specs=[
            pl.BlockSpec((tb, w), lambda b, j: (b * nblk + j, UA_OFF // w)),
            pl.BlockSpec((1, CONV_W - 1, w), lambda b, j: (b, 0, 0)),
            per_seq,
            pl.BlockSpec((CONV_W, w), lambda b, j: (0, 0)),
            vec,
            pl.BlockSpec((w, w), lambda b, j: (0, 0)),
            pl.BlockSpec((w, w), lambda b, j: (0, 0)),
            vec, vec, vec,
        ],
        out_specs=[pl.BlockSpec((tb, w), lambda b, j: (b * nblk + j, 0)), per_seq],
        out_shape=[jax.ShapeDtypeStruct((bsz * t_len, w), y_dtype), jax.ShapeDtypeStruct((bsz, 1, w), F32)],
        scratch_shapes=[pltpu.VMEM((tb + 8, w), F32), pltpu.VMEM((tb, w), F32),
                        pltpu.VMEM((tb, w), F32), pltpu.VMEM((1, w), F32)],
        compiler_params=_cparams(("parallel", "arbitrary")),
        name="rglru",
    )(z_lo, conv_st, h0, cw, cb, wr, wi, br, bi, lam)


def _hgrn_kernel(q_ref, f_ref, i_ref, g_ref, lbraw_ref, ng_ref, s0_ref, y_ref, sfin_ref,
                 st_scr, b_scr, k_scr, *, layer, nch, nv):
    c_rows = HG_CHUNK
    half = HG_SUB // 2
    nsub = -(-nv // HG_SUB)
    j = pl.program_id(1)

    @pl.when(j == 0)
    def _():
        for h in range(HG_HEADS):
            st_scr[h] = s0_ref[0, h].T

    raw = lbraw_ref[...]
    e = jnp.exp(raw - jnp.max(raw, axis=0, keepdims=True))
    soft = e / jnp.sum(e, axis=0, keepdims=True)
    lb_all = jnp.sum(soft[0:layer + 1], axis=0, keepdims=True) - soft[0:1]
    ng = ng_ref[...]

    row = lax.broadcasted_iota(jnp.int32, (c_rows, HG_D), 0)
    col = lax.broadcasted_iota(jnp.int32, (c_rows, HG_D), 1)
    tri = (row >= col).astype(BF16)
    col8 = lax.broadcasted_iota(jnp.int32, (half, HG_D), 1)
    sub_row = lax.broadcasted_iota(jnp.int32, (HG_SUB, HG_D), 0)
    sub_col = lax.broadcasted_iota(jnp.int32, (HG_SUB, HG_D), 1)

    def load(ref, r0, cs):
        x = ref[pl.ds(r0, nv), cs].astype(F32)
        if nv < c_rows:
            x = jnp.concatenate([x, jnp.zeros((c_rows - nv, HG_D), F32)], axis=0)
        return x

    def head(h, r0):
        cs = slice(h * HG_D, (h + 1) * HG_D)
        lb = lb_all[:, cs]
        qz = load(q_ref, r0, cs)
        fz = load(f_ref, r0, cs)
        v = load(i_ref, r0, cs)
        gz = load(g_ref, r0, cs)
        q = qz * _sigmoid(qz)
        f = lb + (1.0 - lb) * _sigmoid(fz)
        logf = jnp.log(f)
        k = 1.0 - f
        if nv < c_rows:
            logf = jnp.where(row < nv, logf, 0.0)
            k = jnp.where(row < nv, k, 0.0)
        hi, mid, lo = _split3(logf)
        b2 = (_dot(tri, hi) + _dot(tri, mid) + _dot(tri, lo)) * LOG2E
        b_scr[h] = b2
        k_scr[h] = k

        st = st_scr[h]
        o = _dot_nt((q * jnp.exp2(b2)).astype(BF16), st.astype(BF16))
        bl = b2[c_rows - 1:c_rows, :]
        kd = k * jnp.exp2(bl - b2)
        st_scr[h] = st * jnp.exp2(bl) + _dot_tn(v.astype(BF16), kd.astype(BF16))

        blocks = []
        for sb in range(nsub):
            lo_r = sb * HG_SUB
            q_i = q[lo_r:lo_r + HG_SUB]
            b_i = b2[lo_r:lo_r + HG_SUB]
            q_top, q_bot = q_i[0:half], q_i[half:HG_SUB]
            b_top, b_bot = b_i[0:half], b_i[half:HG_SUB]
            a_top = jnp.zeros((half, HG_D), F32)
            a_bot = jnp.zeros((half, HG_D), F32)
            for s in range(HG_SUB):
                b_s = b_scr[h, lo_r + s:lo_r + s + 1, :]
                k_s = k_scr[h, lo_r + s:lo_r + s + 1, :]
                if s < half:
                    p = q_top * k_s * jnp.exp2(b_top - b_s)
                    a_top = jnp.where(col8 == lo_r + s, jnp.sum(p, axis=-1, keepdims=True), a_top)
                p = q_bot * k_s * jnp.exp2(b_bot - b_s)
                a_bot = jnp.where(col8 == lo_r + s, jnp.sum(p, axis=-1, keepdims=True), a_bot)
            a_dia = jnp.concatenate([a_top, a_bot], axis=0)
            in_block = (sub_col >= lo_r) & (sub_col <= lo_r + sub_row)
            if sb > 0:
                beta = b_scr[h, lo_r - 1:lo_r, :]
                qs = q_i * jnp.exp2(b_i - beta)
                ks = jnp.concatenate([k[0:lo_r] * jnp.exp2(beta - b2[0:lo_r]),
                                      jnp.zeros((c_rows - lo_r, HG_D), F32)], axis=0)
                blocks.append(jnp.where(in_block, a_dia, _dot_nt(qs.astype(BF16), ks.astype(BF16))))
            else:
                blocks.append(jnp.where(in_block, a_dia, 0.0))
        if nsub * HG_SUB < c_rows:
            blocks.append(jnp.zeros((c_rows - nsub * HG_SUB, c_rows), F32))
        attn = jnp.concatenate(blocks, axis=0)
        o = o + _dot(attn.astype(BF16), v.astype(BF16))

        y = _rms(o, ng) * (gz * _sigmoid(gz))
        y_ref[pl.ds(r0, nv), cs] = y[0:nv].astype(y_ref.dtype)

    def chunk(c, carry):
        r0 = pl.multiple_of(c * nv, nv)
        for h in range(HG_HEADS):
            head(h, r0)
        return carry

    lax.fori_loop(0, nch, chunk, 0)

    @pl.when(j == pl.num_programs(1) - 1)
    def _():
        for h in range(HG_HEADS):
            sfin_ref[0, h] = st_scr[h].T


def _hgrn2(z_lo, z_hi, lb_raw, norm_g, s0, *, layer, bsz, t_len, rb, y_dtype):
    nblk = t_len // rb
    nv = min(rb, HG_CHUNK)
    nch = rb // nv

    def zspec(off):
        return pl.BlockSpec((rb, HG_W), lambda b, j: (b * nblk + j, off // HG_W))

    state = pl.BlockSpec((1, HG_HEADS, HG_D, HG_D), lambda b, j: (b, 0, 0, 0))
    return pl.pallas_call(
        functools.partial(_hgrn_kernel, layer=layer, nch=nch, nv=nv),
        grid=(bsz, nblk),
        in_specs=[
            zspec(QB_OFF), zspec(FB_OFF), zspec(IB_OFF), zspec(GB_OFF),
            pl.BlockSpec((lb_raw.shape[0], HG_W), lambda b, j: (0, 0)),
            pl.BlockSpec((1, HG_D), lambda b, j: (0, 0)),
            state,
        ],
        out_specs=[pl.BlockSpec((rb, HG_W), lambda b, j: (b * nblk + j, 0)), state],
        out_shape=[jax.ShapeDtypeStruct((bsz * t_len, HG_W), y_dtype),
                   jax.ShapeDtypeStruct((bsz, HG_HEADS, HG_D, HG_D), F32)],
        scratch_shapes=[pltpu.VMEM((HG_HEADS, HG_D, HG_D), F32),
                        pltpu.VMEM((HG_HEADS, HG_CHUNK, HG_D), F32),
                        pltpu.VMEM((HG_HEADS, HG_CHUNK, HG_D), F32)],
        compiler_params=_cparams(("parallel", "arbitrary")),
        name="hgrn2",
    )(z_lo, z_hi, z_lo, z_lo, lb_raw, norm_g, s0)


def _head_masks(shape):
    col = lax.broadcasted_iota(jnp.int32, shape, len(shape) - 1)
    return [(col >= h * DIL_DH) & (col < (h + 1) * DIL_DH) for h in range(DIL_HPG)]


def _dil_prompt_kernel(*refs, dil):
    nh = DIL_GW // V7X_LANES
    span = dil * DIL_QB
    nsp = DIL_RB // span
    q_refs, kc_refs, vc_refs, kp_refs, vp_refs = (refs[c * nh:(c + 1) * nh] for c in range(5))
    o_ref, l_ref = refs[5 * nh], refs[5 * nh + 1]
    o_scr, l_scr = refs[5 * nh + 2:5 * nh + 2 + nh], refs[5 * nh + 2 + nh:]
    i = pl.program_id(0)
    qi = lax.broadcasted_iota(jnp.int32, (DIL_QB, 2 * DIL_QB), 0)
    kj = lax.broadcasted_iota(jnp.int32, (DIL_QB, 2 * DIL_QB), 1) - DIL_QB
    dist = qi - kj
    band = (dist >= 0) & (dist <= DIL_QB)
    first_key = jnp.where(i > 0, -DIL_QB, 0)
    band_first = band & (kj >= first_key)
    col = lax.broadcasted_iota(jnp.int32, (DIL_QB, V7X_LANES), 1)
    masks = [(col >= h * DIL_DH) & (col < (h + 1) * DIL_DH) for h in range(V7X_LANES // DIL_DH)]

    def sub(ref, n, r):
        if dil == 1:
            return ref[n * DIL_QB:(n + 1) * DIL_QB, :]
        return ref[pl.ds(n * span + r, DIL_QB, stride=dil), :]

    for n in range(nsp):
        valid = band_first if n == 0 else band
        for r in range(dil):
            for c in range(nh):
                q = sub(q_refs[c], n, r) * (DIL_DH ** -0.5)
                k_prev = sub(kp_refs[c], 0, r) if n == 0 else sub(kc_refs[c], n - 1, r)
                v_prev = sub(vp_refs[c], 0, r) if n == 0 else sub(vc_refs[c], n - 1, r)
                kk = jnp.concatenate([k_prev, sub(kc_refs[c], n, r)], axis=0).astype(BF16)
                vv = jnp.concatenate([v_prev, sub(vc_refs[c], n, r)], axis=0).astype(BF16)
                o_acc = jnp.zeros((DIL_QB, V7X_LANES), F32)
                l_acc = jnp.zeros((DIL_QB, V7X_LANES), F32)
                for mask in masks:
                    qh = jnp.where(mask, q, 0.0).astype(BF16)
                    s = jnp.where(valid, _dot_nt(qh, kk), NEG)
                    m = jnp.max(s, axis=-1, keepdims=True)
                    p = jnp.where(valid, jnp.exp(s - m), 0.0)
                    l = jnp.sum(p, axis=-1, keepdims=True)
                    oh = _dot(p.astype(BF16), vv) * (1.0 / l)
                    o_acc = jnp.where(mask, oh, o_acc)
                    l_acc = jnp.where(mask, m + jnp.log(l), l_acc)
                if dil == 1:
                    o_ref[n * DIL_QB:(n + 1) * DIL_QB, c * V7X_LANES:(c + 1) * V7X_LANES] = o_acc
                    l_ref[n * DIL_QB:(n + 1) * DIL_QB, c * V7X_LANES:(c + 1) * V7X_LANES] = l_acc
                else:
                    o_scr[c][pl.ds(n * span + r, DIL_QB, stride=dil), :] = o_acc
                    l_scr[c][pl.ds(n * span + r, DIL_QB, stride=dil), :] = l_acc
    if dil > 1:
        for c in range(nh):
            o_ref[:, c * V7X_LANES:(c + 1) * V7X_LANES] = o_scr[c][...]
            l_ref[:, c * V7X_LANES:(c + 1) * V7X_LANES] = l_scr[c][...]


def _dilated_prompt(z_hi, g, t_len):
    _, dil = DIL_GROUPS[g]
    span = dil * DIL_QB
    nsp = DIL_RB // span
    nb = t_len // DIL_RB
    nh = DIL_GW // V7X_LANES
    qc, kc, vc = ((off + g * DIL_GW) // V7X_LANES for off in (QC_OFF, KC_OFF, VC_OFF))
    cur = lambda c0: [pl.BlockSpec((DIL_RB, V7X_LANES), lambda i, c=c0 + k: (i, c)) for k in range(nh)]
    prev = lambda c0: [pl.BlockSpec((span, V7X_LANES), lambda i, c=c0 + k: (jnp.maximum(i * nsp - 1, 0), c))
                       for k in range(nh)]
    out = pl.BlockSpec((DIL_RB, DIL_GW), lambda i: (i, 0))
    scr_rows = DIL_RB if dil > 1 else V7X_SUBLANES
    return pl.pallas_call(
        functools.partial(_dil_prompt_kernel, dil=dil),
        grid=(nb,),
        in_specs=cur(qc) + cur(kc) + cur(vc) + prev(kc) + prev(vc),
        out_specs=[out, out],
        out_shape=[jax.ShapeDtypeStruct((t_len, DIL_GW), F32)] * 2,
        scratch_shapes=[pltpu.VMEM((scr_rows, V7X_LANES), F32)] * (2 * nh),
        compiler_params=_cparams(("parallel",)),
        name=f"dil_prompt_{g}",
    )(*([z_hi] * (5 * nh)))


def _dil_decode_kernel(q_ref, kn_ref, vn_ref, kvt_ref, *rest, dil, wc, t_len):
    o_ref, l_ref, co_ref = rest[-3:]
    nq = DIL_HPG * t_len
    q = q_ref[...] * (DIL_DH ** -0.5)
    masks = _head_masks((t_len, DIL_GW))
    q4 = jnp.concatenate([jnp.where(masks[h], q, 0.0) for h in range(DIL_HPG)], axis=0).astype(BF16)
    pad = jnp.zeros((V7X_LANES - t_len, DIL_GW), F32)
    kn_f = jnp.concatenate([kn_ref[...], pad], axis=0)
    vn_f = jnp.concatenate([vn_ref[...], pad], axis=0)
    kn = kn_f.astype(BF16)
    vn = vn_f.astype(BF16)
    kt = kvt_ref[0, 0, 0:DIL_GW, :]
    vt = kvt_ref[0, 0, DIL_GW:2 * DIL_GW, :]

    s_c = _dot(q4, kt.astype(BF16))
    s_n = _dot_nt(q4, kn)
    t_c = lax.broadcasted_iota(jnp.int32, (nq, wc), 0) & (t_len - 1)
    p_c = lax.broadcasted_iota(jnp.int32, (nq, wc), 1)
    valid_c = (((wc + t_c - p_c) & (dil - 1)) == 0) & (p_c >= t_c)
    t_n = lax.broadcasted_iota(jnp.int32, (nq, V7X_LANES), 0) & (t_len - 1)
    u_n = lax.broadcasted_iota(jnp.int32, (nq, V7X_LANES), 1)
    valid_n = (u_n <= t_n) & (((t_n - u_n) & (dil - 1)) == 0)
    s_c = jnp.where(valid_c, s_c, NEG)
    s_n = jnp.where(valid_n, s_n, NEG)
    m = jnp.maximum(jnp.max(s_c, axis=-1, keepdims=True), jnp.max(s_n, axis=-1, keepdims=True))
    e_c = jnp.where(valid_c, jnp.exp(s_c - m), 0.0)
    e_n = jnp.where(valid_n, jnp.exp(s_n - m), 0.0)
    l = jnp.sum(e_c, axis=-1, keepdims=True) + jnp.sum(e_n, axis=-1, keepdims=True)
    o4 = (_dot_nt(e_c.astype(BF16), vt.astype(BF16)) + _dot(e_n.astype(BF16), vn)) * (1.0 / l)
    lse4 = m + jnp.log(l)
    o = jnp.zeros((t_len, DIL_GW), F32)
    lse = jnp.zeros((t_len, DIL_GW), F32)
    for h in range(DIL_HPG):
        o = jnp.where(masks[h], o4[h * t_len:(h + 1) * t_len], o)
        lse = jnp.where(masks[h], lse4[h * t_len:(h + 1) * t_len], lse)
    o_ref[...] = o
    l_ref[...] = lse

    kv_new = jnp.concatenate([kn_f, vn_f], axis=1)
    rr = lax.broadcasted_iota(jnp.int32, (V7X_LANES, V7X_LANES), 0)
    cc = lax.broadcasted_iota(jnp.int32, (V7X_LANES, V7X_LANES), 1)
    place = ((cc == rr + (V7X_LANES - t_len)) & (rr < t_len)).astype(BF16)
    hi, mid, lo = _split3(kv_new)
    new_t = _dot_tn(hi, place) + _dot_tn(mid, place) + _dot_tn(lo, place)

    rolled = pltpu.roll(kvt_ref[0, 0], wc - t_len, axis=1)
    if wc > V7X_LANES:
        co_ref[0, 0, :, 0:wc - V7X_LANES] = rolled[:, 0:wc - V7X_LANES]
    lane = lax.broadcasted_iota(jnp.int32, (2 * DIL_GW, V7X_LANES), 1)
    co_ref[0, 0, :, wc - V7X_LANES:wc] = jnp.where(lane >= V7X_LANES - t_len, new_t,
                                                   rolled[:, wc - V7X_LANES:wc])


def _dilated_decode(z_hi, kvt, prev_out, g, layer, bsz, t_len):
    _, dil = DIL_GROUPS[g]
    depth, _, _, wc = kvt.shape
    qc, kc, vc = (QC_OFF // DIL_GW + g, KC_OFF // DIL_GW + g, VC_OFF // DIL_GW + g)
    zs = lambda c: pl.BlockSpec((t_len, DIL_GW), lambda b: (b, c))
    out = pl.BlockSpec((t_len, DIL_GW), lambda b: (b, 0))
    in_specs = [zs(qc), zs(kc), zs(vc),
                pl.BlockSpec((1, 1, 2 * DIL_GW, wc), lambda b: (layer, b, 0, 0))]
    args = [z_hi, z_hi, z_hi, kvt]
    aliases = {}
    if prev_out is not None:
        in_specs.append(pl.BlockSpec(memory_space=pl.ANY))
        args.append(prev_out)
        aliases = {len(args) - 1: 2}
    return pl.pallas_call(
        functools.partial(_dil_decode_kernel, dil=dil, wc=wc, t_len=t_len),
        grid=(bsz,),
        in_specs=in_specs,
        out_specs=[out, out, pl.BlockSpec((1, 1, 2 * DIL_GW, wc), lambda b: (layer, b, 0, 0))],
        out_shape=[jax.ShapeDtypeStruct((bsz * t_len, DIL_GW), F32)] * 2
        + [jax.ShapeDtypeStruct((depth, bsz, 2 * DIL_GW, wc), F32)],
        input_output_aliases=aliases,
        compiler_params=_cparams(("parallel",)),
        name=f"dil_decode_{g}",
    )(*args)


def _mix_kernel(ya_ref, yb_ref, o0_ref, o1_ref, o2_ref, l0_ref, l1_ref, l2_ref, gate_ref, bg_ref, x_ref,
                wa_ref, wb_ref, wc_ref, wo_ref, npost_ref, npre_ref, x1_ref, hm_ref):
    l0, l1, l2 = l0_ref[...], l1_ref[...], l2_ref[...]
    m = jnp.maximum(jnp.maximum(l0, l1), l2)
    e0, e1, e2 = jnp.exp(l0 - m), jnp.exp(l1 - m), jnp.exp(l2 - m)
    yc = (e0 * o0_ref[...] + e1 * o1_ref[...] + e2 * o2_ref[...]) * (1.0 / (e0 + e1 + e2))
    gates = _sigmoid(gate_ref[...].astype(F32) + bg_ref[...])
    mixed = (gates[:, 0:D_MODEL] * _dot(ya_ref[...].astype(BF16), wa_ref[...])
             + gates[:, D_MODEL:2 * D_MODEL] * _dot(yb_ref[...].astype(BF16), wb_ref[...])
             + gates[:, 2 * D_MODEL:3 * D_MODEL] * _dot(yc.astype(BF16), wc_ref[...]))
    t = _dot(mixed.astype(BF16), wo_ref[...])
    x1 = x_ref[...] + _rms(t, npost_ref[...])
    x1_ref[...] = x1
    hm_ref[...] = _rms(x1, npre_ref[...]).astype(BF16)


def _mix(ya, yb, os_, ls_, z_lo, bg, x, wa, wb, wc, wo, npost, npre, tm):
    m = x.shape[0]
    rows = lambda w: pl.BlockSpec((tm, w), lambda i: (i, 0))
    full = lambda a: pl.BlockSpec(a.shape, lambda i: (0,) * a.ndim)
    return pl.pallas_call(
        _mix_kernel,
        grid=(m // tm,),
        in_specs=[rows(LRU_WIDTH), rows(HG_W)] + [rows(DIL_GW)] * 6
        + [rows(GATE_W), full(bg), rows(D_MODEL), full(wa), full(wb), full(wc), full(wo), full(npost), full(npre)],
        out_specs=[rows(D_MODEL), rows(D_MODEL)],
        out_shape=[jax.ShapeDtypeStruct((m, D_MODEL), F32), jax.ShapeDtypeStruct((m, D_MODEL), BF16)],
        compiler_params=_cparams(("parallel",)),
        name="mix",
    )(ya, yb, *os_, *ls_, z_lo, bg, x, wa, wb, wc, wo, npost, npre)


def _mlp_kernel(hm_ref, x1_ref, wu_ref, wd_ref, npost_ref, x2_ref, acc_scr):
    k = pl.program_id(1)

    @pl.when(k == 0)
    def _():
        acc_scr[...] = jnp.zeros_like(acc_scr)

    up = jnp.maximum(_dot(hm_ref[...], wu_ref[...]), 0.0)
    acc_scr[...] += _dot((up * up).astype(BF16), wd_ref[...])

    @pl.when(k == pl.num_programs(1) - 1)
    def _():
        x2_ref[...] = x1_ref[...] + _rms(acc_scr[...], npost_ref[...])


def _mlp(hm, x1, wu, wd, npost, tm, tf):
    m = x1.shape[0]
    return pl.pallas_call(
        _mlp_kernel,
        grid=(m // tm, D_FF // tf),
        in_specs=[
            pl.BlockSpec((tm, D_MODEL), lambda i, k: (i, 0)),
            pl.BlockSpec((tm, D_MODEL), lambda i, k: (i, 0)),
            pl.BlockSpec((D_MODEL, tf), lambda i, k: (0, k)),
            pl.BlockSpec((tf, D_MODEL), lambda i, k: (k, 0)),
            pl.BlockSpec((1, D_MODEL), lambda i, k: (0, 0)),
        ],
        out_specs=pl.BlockSpec((tm, D_MODEL), lambda i, k: (i, 0)),
        out_shape=jax.ShapeDtypeStruct((m, D_MODEL), F32),
        scratch_shapes=[pltpu.VMEM((tm, D_MODEL), F32)],
        compiler_params=_cparams(("parallel", "arbitrary")),
        name="mlp",
    )(hm, x1, wu, wd, npost)


def _permute_w_in(w):
    d = w.shape[0]
    u_a, q_b, f_b, i_b, g_b = (w[:, c * 512:(c + 1) * 512] for c in range(5))
    qkv = w[:, 2560:2560 + 3 * DIL_W]
    gate = w[:, 2560 + 3 * DIL_W:]
    pad = jnp.zeros((d, N_HI - HG_W - 3 * DIL_W), w.dtype)
    return jnp.concatenate([gate, u_a, q_b, i_b, g_b, f_b, qkv, pad], axis=1).astype(BF16)


def _block_diag(w):
    n, d, _ = w.shape
    eye = jnp.eye(n, dtype=w.dtype)
    return jnp.einsum("nde,nm->ndme", w, eye).reshape(n * d, n * d).astype(BF16)


def _row(v):
    return v.reshape(1, -1)


def _layer(x, lw, layer, *, bsz, t_len, conv_st, h0, s0, caches, cache_outs, tiles):
    act = tiles["act_dtype"]
    z_lo, z_hi = _inproj(x, lw["n_mix_pre"], lw["w_in"], tiles["a_tm"], act)
    ya, h_last = _rglru(z_lo, conv_st, h0, lw["conv_w"], lw["conv_b"], lw["wr"], lw["wi"], lw["br"], lw["bi"],
                        lw["lam"], bsz=bsz, t_len=t_len, tb=tiles["lru_tb"], first_pos_is_zero=caches is None,
                        y_dtype=act)
    yb, s_fin = _hgrn2(z_lo, z_hi, lw["lb_raw"], lw["hg_norm"], s0, layer=layer, bsz=bsz, t_len=t_len,
                       rb=tiles["hg_rb"], y_dtype=act)
    os_, ls_, new_caches = [], [], []
    for g in range(len(DIL_GROUPS)):
        if caches is None:
            o, l = _dilated_prompt(z_hi, g, t_len)
        else:
            o, l, c_new = _dilated_decode(z_hi, caches[g], cache_outs[g], g, layer, bsz, t_len)
            new_caches.append(c_new)
        os_.append(o)
        ls_.append(l)
    x1, hm = _mix(ya, yb, os_, ls_, z_lo, lw["b_gate"], x, lw["w_br_a"], lw["w_br_b"], lw["w_br_c"],
                  lw["w_out"], lw["n_mix_post"], lw["n_mlp_pre"], tiles["mix_tm"])
    x2 = _mlp(hm, x1, lw["w_up"], lw["w_down"], lw["n_mlp_post"], tiles["mlp_tm"], tiles["mlp_tf"])
    return x2, z_lo, z_hi, h_last, s_fin, new_caches


def kernel(x_prompt, x_sample, state_conv, state_lru, state_hgrn, cache_win128, cache_win512, cache_win2048, norm_mix_pre, norm_mix_post, norm_mlp_pre, norm_mlp_post, w_in, b_gate, conv_w, conv_b, lru_wr, lru_br, lru_wi, lru_bi, lru_lambda, hgrn_lb_raw, hgrn_norm, w_br_a, w_br_b, w_br_c, w_out, w_mlp_up, w_mlp_down):
    depth = w_in.shape[0]
    bp, t_p, _ = x_prompt.shape
    bs, t_s, _ = x_sample.shape
    assert bp == 1 and t_p % DIL_RB == 0 and t_s == V7X_SUBLANES
    caches_in = (cache_win128, cache_win512, cache_win2048)
    caches_t = [c.transpose(0, 1, 3, 4, 5, 2).reshape(depth, bs, 2 * DIL_GW, c.shape[2]) for c in caches_in]

    m_s = bs * t_s
    tiles_p = dict(a_tm=min(2048, t_p), lru_tb=512, hg_rb=1024, mix_tm=512, mlp_tm=1024, mlp_tf=1024,
                   act_dtype=BF16)
    tiles_s = dict(a_tm=m_s, lru_tb=t_s, hg_rb=t_s, mix_tm=m_s, mlp_tm=m_s, mlp_tf=1024, act_dtype=F32)

    xp = x_prompt.reshape(bp * t_p, D_MODEL)
    xs = x_sample.reshape(bs * t_s, D_MODEL)
    outs_p = [[] for _ in range(6)]
    outs_s = [[] for _ in range(3)]
    cache_outs = [None] * len(DIL_GROUPS)
    for l in range(depth):
        lw = dict(
            n_mix_pre=_row(norm_mix_pre[l]), n_mix_post=_row(norm_mix_post[l]),
            n_mlp_pre=_row(norm_mlp_pre[l]), n_mlp_post=_row(norm_mlp_post[l]),
            w_in=_permute_w_in(w_in[l]), b_gate=_row(b_gate[l]), conv_w=conv_w[l], conv_b=_row(conv_b[l]),
            wr=_block_diag(lru_wr[l]), wi=_block_diag(lru_wi[l]), br=_row(lru_br[l]), bi=_row(lru_bi[l]),
            lam=_row(lru_lambda[l]), lb_raw=hgrn_lb_raw, hg_norm=_row(hgrn_norm[l]),
            w_br_a=w_br_a[l].astype(BF16), w_br_b=w_br_b[l].astype(BF16), w_br_c=w_br_c[l].astype(BF16),
            w_out=w_out[l].astype(BF16), w_up=w_mlp_up[l].astype(BF16), w_down=w_mlp_down[l].astype(BF16))

        xp, zp_lo, zp_hi, hl_p, sp, _ = _layer(
            xp, lw, l, bsz=bp, t_len=t_p, conv_st=jnp.zeros((bp, CONV_W - 1, LRU_WIDTH), F32),
            h0=jnp.zeros((bp, 1, LRU_WIDTH), F32), s0=jnp.zeros((bp, HG_HEADS, HG_D, HG_D), F32),
            caches=None, cache_outs=None, tiles=tiles_p)
        xs, zs_lo, _, hl_s, ss, cache_outs = _layer(
            xs, lw, l, bsz=bs, t_len=t_s, conv_st=state_conv[l], h0=state_lru[l][:, None, :],
            s0=state_hgrn[l], caches=caches_t, cache_outs=cache_outs, tiles=tiles_s)

        outs_p[0].append(zp_lo[t_p - (CONV_W - 1):, UA_OFF:UA_OFF + LRU_WIDTH].astype(F32)[None])
        outs_p[1].append(hl_p[:, 0])
        outs_p[2].append(sp)
        for g, (window, _) in enumerate(DIL_GROUPS):
            w = min(window, t_p)
            kv = jnp.stack([zp_hi[t_p - w:, KC_OFF + g * DIL_GW:KC_OFF + (g + 1) * DIL_GW],
                            zp_hi[t_p - w:, VC_OFF + g * DIL_GW:VC_OFF + (g + 1) * DIL_GW]], axis=1)
            outs_p[3 + g].append(kv.reshape(1, w, 2, DIL_HPG, DIL_DH))
        outs_s[0].append(zs_lo.reshape(bs, t_s, N_LO)[:, t_s - (CONV_W - 1):, UA_OFF:UA_OFF + LRU_WIDTH])
        outs_s[1].append(hl_s[:, 0])
        outs_s[2].append(ss)

    res_p = [jnp.stack(v, axis=0) for v in outs_p]
    res_s = [jnp.stack(v, axis=0) for v in outs_s]
    win_s = [c.reshape(depth, bs, 2, DIL_HPG, DIL_DH, c.shape[-1]).transpose(0, 1, 5, 2, 3, 4) for c in cache_outs]
    return (xp.reshape(bp, t_p, D_MODEL), xs.reshape(bs, t_s, D_MODEL),
            res_p[0], res_s[0], res_p[1], res_s[1], res_p[2], res_s[2],
            res_p[3], win_s[0], res_p[4], win_s[1], res_p[5], win_s[2])
```
